```python
import math
import jax
import jax.numpy as jnp
from jax import lax
import numpy as np

D_MODEL = 1024
BATCH = 8
SEQ = 2048
DEPTH = 4
DEC_BATCH = 128
DEC_SEQ = 8
PAST_LEN = 2048
PAGE_SIZE = 128

N_MIXERS = 2
N_NSA_LAYERS = (DEPTH + 1) // 2
N_GDN_LAYERS = DEPTH // 2

NSA_HEAD_DIM = 64
NSA_HEADS = D_MODEL // NSA_HEAD_DIM
NSA_KV_GROUPS = 4
NSA_Q_WIDTH = NSA_HEADS * NSA_HEAD_DIM
NSA_KV_WIDTH = NSA_KV_GROUPS * NSA_HEAD_DIM
NSA_N_KV = 6
NSA_CACHED = 4
NSA_IN = NSA_Q_WIDTH + NSA_N_KV * NSA_KV_WIDTH + 3 * NSA_HEADS
CMP_BLOCK = 32
CMP_STRIDE = 16
SEL_BLOCK = 64
SEL_TOP_N = 8
WINDOW = 512
NSA_Q_BLOCK = 64
FORCE_BONUS = 1e4

REL_BUCKETS = 32
REL_MAX_DIST = 128

GDN_HEAD_K = 128
GDN_HEAD_V = 128
GDN_QK_HEADS = D_MODEL // GDN_HEAD_K
GDN_V_HEADS = 2 * GDN_QK_HEADS
GDN_QK_W = GDN_QK_HEADS * GDN_HEAD_K
GDN_V_W = GDN_V_HEADS * GDN_HEAD_V
GDN_CONV = 4
GDN_CONV_CH = 2 * GDN_QK_W + GDN_V_W
GDN_IN = GDN_CONV_CH + GDN_V_W + 2 * GDN_V_HEADS
GDN_CHUNK = 64

FFN_HIDDEN = 4 * D_MODEL
RMS_EPS = 1e-6
NEG_INF = -1e30

kernel_name = 'nsa_gdn_hybrid_step'


def rmsnorm(x, w):
    xf = x.astype(jnp.float32)
    y = xf * lax.rsqrt(jnp.mean(xf * xf, axis=-1, keepdims=True) + RMS_EPS)
    return (y * w.astype(jnp.float32)).astype(x.dtype)


def l2norm(x):
    return x * lax.rsqrt(jnp.sum(x * x, axis=-1, keepdims=True) + 1e-6)


def masked_softmax(logits, mask):
    logits = jnp.where(mask, logits, NEG_INF)
    m = jnp.max(logits, axis=-1, keepdims=True)
    e = jnp.where(mask, jnp.exp(logits - m), 0.0)
    s = jnp.sum(e, axis=-1, keepdims=True)
    return e / jnp.where(s > 0, s, 1.0)


def rel_bucket(dist):
    n = jnp.maximum(dist, 0)
    max_exact = REL_BUCKETS // 2
    nf = jnp.maximum(n, 1).astype(jnp.float32)
    large = max_exact + (jnp.log(nf / max_exact) / math.log(REL_MAX_DIST / max_exact)
                         * (REL_BUCKETS - max_exact)).astype(jnp.int32)
    return jnp.where(n < max_exact, n, jnp.minimum(large, REL_BUCKETS - 1))


def cmp_sel_overlap(nc, ns):
    c0 = jnp.arange(nc) * CMP_STRIDE
    s0 = jnp.arange(ns) * SEL_BLOCK
    ov = jnp.minimum(c0[:, None] + CMP_BLOCK, s0[None, :] + SEL_BLOCK) - jnp.maximum(c0[:, None], s0[None, :])
    return jnp.maximum(ov, 0).astype(jnp.float32) / CMP_BLOCK


def compress(x, w1, w2, pe):
    n, l, g, dh = x.shape
    r = CMP_BLOCK // CMP_STRIDE
    n_str = l // CMP_STRIDE
    nc = n_str - r + 1
    xs = x[:, :n_str * CMP_STRIDE].reshape(n, n_str, CMP_STRIDE, g, dh)
    w1r = w1.reshape(r, CMP_STRIDE, dh, dh)
    hid = jnp.einsum('id,ide->e', pe, w1)
    for j in range(r):
        hid = hid + jnp.einsum('nbigd,ide->nbge', xs[:, j:j + nc], w1r[j])
    return jnp.einsum('nbge,ef->nbgf', jax.nn.silu(hid), w2)


def to_blocks(x):
    n, l, g, dh = x.shape
    ns = -(-l // SEL_BLOCK)
    x = jnp.pad(x, ((0, 0), (0, ns * SEL_BLOCK - l), (0, 0), (0, 0)))
    return x.reshape(n, ns, SEL_BLOCK, g, dh).transpose(0, 3, 1, 2, 4)


def nsa_project(xn, w_in):
    n, t, _ = xn.shape
    p = xn @ w_in
    q = p[..., :NSA_Q_WIDTH].reshape(n, t, NSA_HEADS, NSA_HEAD_DIM)
    kv_end = NSA_Q_WIDTH + NSA_N_KV * NSA_KV_WIDTH
    kv = p[..., NSA_Q_WIDTH:kv_end].reshape(n, t, NSA_N_KV, NSA_KV_GROUPS, NSA_HEAD_DIM)
    gates = jax.nn.sigmoid(p[..., kv_end:].astype(jnp.float32)).reshape(n, t, NSA_HEADS, 3)
    return q, kv, gates


def nsa_core(q, gates, qpos, kc, vc, cmp_end, ks_blk, vs_blk, kw, vw, kwpos, rel_bias):
    n, tq, h, dh = q.shape
    g = kc.shape[2]
    hpg = h // g
    nc = kc.shape[1]
    ns, sb = ks_blk.shape[2], ks_blk.shape[3]
    qg = (q.astype(jnp.float32) * dh ** -0.5).reshape(n, tq, g, hpg, dh)
    tbl = rel_bias.astype(jnp.float32)

    dc = qpos[:, None] - cmp_end[None, :]
    bc = tbl[rel_bucket(dc)].reshape(tq, nc, g, hpg).transpose(0, 2, 3, 1)
    lc = jnp.einsum('ntghd,ncgd->ntghc', qg, kc) + bc
    pc = masked_softmax(lc, (dc >= 0)[:, None, None, :])
    oc = jnp.einsum('ntghc,ncgd->ntghd', pc, vc)

    imp = jnp.einsum('ntgc,cs->ntgs', pc.sum(axis=3), cmp_sel_overlap(nc, ns))
    blk = jnp.arange(ns)
    cur = (qpos // sb)[:, None]
    forced = ((blk == 0) | (blk == cur) | (blk == cur - 1)).astype(jnp.float32)
    valid = blk[None, :] * sb <= qpos[:, None]
    score = jnp.where(valid[:, None, :], imp + FORCE_BONUS * forced[:, None, :], -1e9)
    _, idx = lax.top_k(score, min(SEL_TOP_N, ns))
    kk = idx.shape[-1]
    b_i = jnp.arange(n)[:, None, None, None]
    g_i = jnp.arange(g)[None, None, :, None]
    ksel = ks_blk[b_i, g_i, idx]
    vsel = vs_blk[b_i, g_i, idx]
    kpos = idx[..., None] * sb + jnp.arange(sb)
    ds = qpos[None, :, None, None, None] - kpos
    bs = tbl.reshape(REL_BUCKETS, g, hpg)[rel_bucket(ds), g_i[..., None]]
    ls = jnp.einsum('ntghd,ntgjsd->ntghjs', qg, ksel) + jnp.moveaxis(bs, -1, 3)
    ps = masked_softmax(ls.reshape(n, tq, g, hpg, kk * sb),
                        (ds >= 0).reshape(n, tq, g, 1, kk * sb))
    osel = jnp.einsum('ntghjs,ntgjsd->ntghd', ps.reshape(n, tq, g, hpg, kk, sb), vsel)

    tw = kwpos.shape[0]
    dw = qpos[:, None] - kwpos[None, :]
    mw = (dw >= 0) & (dw < WINDOW) & (kwpos >= 0)[None, :]
    bw = tbl[rel_bucket(dw)].reshape(tq, tw, g, hpg).transpose(0, 2, 3, 1)
    lw = jnp.einsum('ntghd,nkgd->ntghk', qg, kw) + bw
    pw = masked_softmax(lw, mw[:, None, None, :])
    ow = jnp.einsum('ntghk,nkgd->ntghd', pw, vw)

    gt = gates.reshape(n, tq, g, hpg, 3)
    o = gt[..., 0:1] * oc + gt[..., 1:2] * osel + gt[..., 2:3] * ow
    return o.reshape(n, tq, h * dh)


def nsa_prompt(xn, w_in, w_out, cw1, cw2, cpe, rel_bias):
    n, t, _ = xn.shape
    q, kv, gates = nsa_project(xn, w_in)
    kc = compress(kv[:, :, 0], cw1[0], cw2[0], cpe[0])
    vc = compress(kv[:, :, 1], cw1[1], cw2[1], cpe[1])
    cmp_end = jnp.arange(kc.shape[1]) * CMP_STRIDE + CMP_BLOCK - 1
    ks_blk = to_blocks(kv[:, :, 2])
    vs_blk = to_blocks(kv[:, :, 3])
    pad = ((0, 0), (WINDOW, 0), (0, 0), (0, 0))
    kw_pad = jnp.pad(kv[:, :, 4], pad)
    vw_pad = jnp.pad(kv[:, :, 5], pad)
    nq = t // NSA_Q_BLOCK
    qc = q.reshape(n, nq, NSA_Q_BLOCK, NSA_HEADS, NSA_HEAD_DIM).swapaxes(0, 1)
    gc = gates.reshape(n, nq, NSA_Q_BLOCK, NSA_HEADS, 3).swapaxes(0, 1)

    def block(args):
        c, qb, gb = args
        start = c * NSA_Q_BLOCK
        kwb = lax.dynamic_slice_in_dim(kw_pad, start, WINDOW + NSA_Q_BLOCK, axis=1)
        vwb = lax.dynamic_slice_in_dim(vw_pad, start, WINDOW + NSA_Q_BLOCK, axis=1)
        qpos = start + jnp.arange(NSA_Q_BLOCK)
        kwpos = start - WINDOW + jnp.arange(WINDOW + NSA_Q_BLOCK)
        return nsa_core(qb, gb, qpos, kc, vc, cmp_end, ks_blk, vs_blk, kwb, vwb, kwpos, rel_bias)

    o = lax.map(block, (jnp.arange(nq), qc, gc))
    o = o.swapaxes(0, 1).reshape(n, t, NSA_Q_WIDTH)
    y = o.astype(xn.dtype) @ w_out
    wbuf = min(WINDOW, t)
    return y, kv[:, :, :NSA_CACHED], kv[:, t - wbuf:, NSA_CACHED:]


def nsa_sample(xn, cache_kv_l, cache_win_l, page_table, w_in, w_out, cw1, cw2, cpe, rel_bias):
    n, t, _ = xn.shape
    q, kv, gates = nsa_project(xn, w_in)
    past = cache_kv_l[page_table]
    past_len = past.shape[1] * past.shape[2]
    past = past.reshape(n, past_len, NSA_CACHED, NSA_KV_GROUPS, NSA_HEAD_DIM)
    full = jnp.concatenate([past, kv[:, :, :NSA_CACHED].astype(past.dtype)], axis=1)
    kc = compress(full[:, :, 0], cw1[0], cw2[0], cpe[0])
    vc = compress(full[:, :, 1], cw1[1], cw2[1], cpe[1])
    cmp_end = jnp.arange(kc.shape[1]) * CMP_STRIDE + CMP_BLOCK - 1
    ks_blk = to_blocks(full[:, :, 2])
    vs_blk = to_blocks(full[:, :, 3])
    wbuf = cache_win_l.shape[1]
    win = jnp.concatenate([cache_win_l, kv[:, :, NSA_CACHED:].astype(cache_win_l.dtype)], axis=1)
    kwpos = past_len - wbuf + jnp.arange(wbuf + t)
    qpos = past_len + jnp.arange(t)
    o = nsa_core(q, gates, qpos, kc, vc, cmp_end, ks_blk, vs_blk,
                 win[:, :, 0], win[:, :, 1], kwpos, rel_bias)
    y = o.astype(xn.dtype) @ w_out
    return y, kv[:, :, :NSA_CACHED], win[:, -wbuf:]


def chunk_gated_delta(q, k, v, beta, g, s0):
    f32 = jnp.float32
    n, t, h, dk = q.shape
    dv = v.shape[-1]
    c = min(GDN_CHUNK, t)
    nc = -(-t // c)
    pad = nc * c - t

    def prep(x):
        x = jnp.pad(x.astype(f32), [(0, 0), (0, pad)] + [(0, 0)] * (x.ndim - 2))
        x = x.reshape((n, nc, c) + x.shape[2:])
        return jnp.moveaxis(x, (1, 2), (0, 3))

    qc, kc, vc, bc, gc = prep(q), prep(k), prep(v), prep(beta), prep(g)
    gam = jnp.cumsum(gc, axis=-1)
    incl = jnp.tril(jnp.ones((c, c), bool))
    strict = jnp.tril(jnp.ones((c, c), bool), -1)
    diff = gam[..., :, None] - gam[..., None, :]
    decay = jnp.where(incl, jnp.exp(jnp.where(incl, diff, 0.0)), 0.0)
    a_mat = jnp.where(strict, bc[..., :, None] * jnp.einsum('...id,...jd->...ij', kc, kc) * decay, 0.0)
    eye = jnp.eye(c, dtype=f32)
    tinv = lax.linalg.triangular_solve(eye + a_mat, jnp.broadcast_to(eye, a_mat.shape),
                                       left_side=True, lower=True)
    value = tinv @ (vc * bc[..., None])
    kcum = tinv @ (kc * (bc * jnp.exp(gam))[..., None])
    qk = jnp.einsum('...id,...jd->...ij', qc, kc) * decay
    q_dec = qc * jnp.exp(gam)[..., None]
    k_dec = kc * jnp.exp(gam[..., -1:] - gam)[..., None]
    g_last = jnp.exp(gam[..., -1])[..., None, None]

    def step(s, xs):
        val, kcm, qkm, qd, kd, gl = xs
        u = val - kcm @ s
        o = qd @ s + qkm @ u
        s = s * gl + jnp.swapaxes(kd, -1, -2) @ u
        return s, o

    s_fin, o = lax.scan(step, s0.astype(f32), (value, kcum, qk, q_dec, k_dec, g_last))
    o = jnp.moveaxis(o, (0, 3), (1, 2)).reshape(n, nc * c, h, dv)[:, :t]
    return o, s_fin


def gdn_mixer(xn, conv_hist, s0, w_in, conv_w, a_log, dt_bias, norm_w, w_out):
    f32 = jnp.float32
    n, t, _ = xn.shape
    p = xn @ w_in
    qkv = p[..., :GDN_CONV_CH]
    z = p[..., GDN_CONV_CH:GDN_CONV_CH + GDN_V_W]
    a = p[..., GDN_CONV_CH + GDN_V_W:GDN_CONV_CH + GDN_V_W + GDN_V_HEADS]
    b = p[..., GDN_CONV_CH + GDN_V_W + GDN_V_HEADS:]
    xc = jnp.concatenate([conv_hist.astype(qkv.dtype), qkv], axis=1)
    conv = xc[:, 0:t] * conv_w[0]
    for i in range(1, GDN_CONV):
        conv = conv + xc[:, i:i + t] * conv_w[i]
    act = jax.nn.silu(conv.astype(f32))
    qh = l2norm(act[..., :GDN_QK_W].reshape(n, t, GDN_QK_HEADS, GDN_HEAD_K)) * GDN_HEAD_K ** -0.5
    kh = l2norm(act[..., GDN_QK_W:2 * GDN_QK_W].reshape(n, t, GDN_QK_HEADS, GDN_HEAD_K))
    vh = act[..., 2 * GDN_QK_W:].reshape(n, t, GDN_V_HEADS, GDN_HEAD_V)
    rep = GDN_V_HEADS // GDN_QK_HEADS
    qh = jnp.repeat(qh, rep, axis=2)
    kh = jnp.repeat(kh, rep, axis=2)
    beta = jax.nn.sigmoid(b.astype(f32))
    gdec = -jnp.exp(a_log.astype(f32)) * jax.nn.softplus(a.astype(f32) + dt_bias.astype(f32))
    o, s_new = chunk_gated_delta(qh, kh, vh, beta, gdec, s0)
    zh = z.reshape(n, t, GDN_V_HEADS, GDN_HEAD_V).astype(f32)
    o = o * lax.rsqrt(jnp.mean(o * o, axis=-1, keepdims=True) + RMS_EPS) * norm_w.astype(f32) * jax.nn.silu(zh)
    y = o.reshape(n, t, GDN_V_W).astype(xn.dtype) @ w_out
    return y, s_new, xc[:, -(GDN_CONV - 1):]


def sqrelu_mlp(x, w1, w2):
    hid = jax.nn.relu(x @ w1)
    return (hid * hid) @ w2


def setup_inputs(seed: int = 0) -> dict:
    key = jax.random.key(seed)
    ks = jax.random.split(key, 24)
    f32 = jnp.float32
    n_pages = PAST_LEN // PAGE_SIZE
    n_used = DEC_BATCH * n_pages
    pool = (5 * n_used + 3) // 4
    wbuf = min(WINDOW, PAST_LEN)

    def nrm(k, shape, scale):
        return scale * jax.random.normal(k, shape, f32)

    page_table = jax.random.permutation(ks[0], pool)[:n_used].reshape(DEC_BATCH, n_pages).astype(jnp.int32)
    dt = jnp.exp(jax.random.uniform(ks[1], (N_GDN_LAYERS, GDN_V_HEADS), f32, math.log(1e-3), math.log(1e-1)))
    return {
        'x_prompt': nrm(ks[2], (BATCH, SEQ, D_MODEL), 1.0),
        'x_sample': nrm(ks[3], (DEC_BATCH, DEC_SEQ, D_MODEL), 1.0),
        'cache_kv': nrm(ks[4], (N_NSA_LAYERS, pool, PAGE_SIZE, NSA_CACHED, NSA_KV_GROUPS, NSA_HEAD_DIM), 1.0),
        'cache_win': nrm(ks[5], (N_NSA_LAYERS, DEC_BATCH, wbuf, 2, NSA_KV_GROUPS, NSA_HEAD_DIM), 1.0),
        'state_ssm': nrm(ks[6], (N_GDN_LAYERS, DEC_BATCH, GDN_V_HEADS, GDN_HEAD_K, GDN_HEAD_V), 0.1),
        'state_conv': nrm(ks[7], (N_GDN_LAYERS, DEC_BATCH, GDN_CONV - 1, GDN_CONV_CH), 1.0),
        'page_table': page_table,
        'rel_bias': nrm(ks[8], (REL_BUCKETS, NSA_HEADS), 0.5),
        'norm_mix': 1.0 + nrm(ks[9], (DEPTH, D_MODEL), 0.02),
        'norm_ffn': 1.0 + nrm(ks[10], (DEPTH, D_MODEL), 0.02),
        'norm_final': 1.0 + nrm(ks[11], (D_MODEL,), 0.02),
        'nsa_w_in': nrm(ks[12], (N_NSA_LAYERS, D_MODEL, NSA_IN), D_MODEL ** -0.5),
        'nsa_w_out': nrm(ks[13], (N_NSA_LAYERS, NSA_Q_WIDTH, D_MODEL), NSA_Q_WIDTH ** -0.5),
        'nsa_cmp_w1': nrm(ks[14], (N_NSA_LAYERS, 2, CMP_BLOCK, NSA_HEAD_DIM, NSA_HEAD_DIM), (CMP_BLOCK * NSA_HEAD_DIM) ** -0.5),
        'nsa_cmp_w2': nrm(ks[15], (N_NSA_LAYERS, 2, NSA_HEAD_DIM, NSA_HEAD_DIM), NSA_HEAD_DIM ** -0.5),
        'nsa_cmp_pe': nrm(ks[16], (N_NSA_LAYERS, 2, CMP_BLOCK, NSA_HEAD_DIM), 0.1),
        'gdn_w_in': nrm(ks[17], (N_GDN_LAYERS, D_MODEL, GDN_IN), D_MODEL ** -0.5),
        'gdn_conv_w': nrm(ks[18], (N_GDN_LAYERS, GDN_CONV, GDN_CONV_CH), 0.5),
        'gdn_a_log': jnp.log(jax.random.uniform(ks[19], (N_GDN_LAYERS, GDN_V_HEADS), f32, 1.0, 16.0)),
        'gdn_dt_bias': dt + jnp.log(-jnp.expm1(-dt)),
        'gdn_norm_w': 1.0 + nrm(ks[20], (N_GDN_LAYERS, GDN_HEAD_V), 0.02),
        'gdn_w_out': nrm(ks[21], (N_GDN_LAYERS, GDN_V_W, D_MODEL), GDN_V_W ** -0.5),
        'ffn_w1': nrm(ks[22], (DEPTH, D_MODEL, FFN_HIDDEN), D_MODEL ** -0.5),
        'ffn_w2': nrm(ks[23], (DEPTH, FFN_HIDDEN, D_MODEL), FFN_HIDDEN ** -0.5),
    }


def reference(x_prompt, x_sample, cache_kv, cache_win, state_ssm, state_conv, page_table,
              rel_bias, norm_mix, norm_ffn, norm_final,
              nsa_w_in, nsa_w_out, nsa_cmp_w1, nsa_cmp_w2, nsa_cmp_pe,
              gdn_w_in, gdn_conv_w, gdn_a_log, gdn_dt_bias, gdn_norm_w, gdn_w_out,
              ffn_w1, ffn_w2):
    xp, xs = x_prompt, x_sample
    kv_p, kv_s, win_p, win_s = [], [], [], []
    ssm_p, ssm_s, conv_p, conv_s = [], [], [], []
    for i in range(DEPTH):
        li = i // N_MIXERS
        hp = rmsnorm(xp, norm_mix[i])
        hs = rmsnorm(xs, norm_mix[i])
        if i % N_MIXERS == 0:
            yp, rows_p, wp = nsa_prompt(hp, nsa_w_in[li], nsa_w_out[li], nsa_cmp_w1[li],
                                        nsa_cmp_w2[li], nsa_cmp_pe[li], rel_bias)
            ys, rows_s, ws = nsa_sample(hs, cache_kv[li], cache_win[li], page_table, nsa_w_in[li],
                                        nsa_w_out[li], nsa_cmp_w1[li], nsa_cmp_w2[li],
                                        nsa_cmp_pe[li], rel_bias)
            kv_p.append(rows_p)
            kv_s.append(rows_s)
            win_p.append(wp)
            win_s.append(ws)
        else:
            n = xp.shape[0]
            h0 = jnp.zeros((n, GDN_CONV - 1, GDN_CONV_CH), xp.dtype)
            s0 = jnp.zeros((n, GDN_V_HEADS, GDN_HEAD_K, GDN_HEAD_V), jnp.float32)
            yp, sp, cp = gdn_mixer(hp, h0, s0, gdn_w_in[li], gdn_conv_w[li], gdn_a_log[li],
                                   gdn_dt_bias[li], gdn_norm_w[li], gdn_w_out[li])
            ys, ss, cs = gdn_mixer(hs, state_conv[li], state_ssm[li], gdn_w_in[li], gdn_conv_w[li],
                                   gdn_a_log[li], gdn_dt_bias[li], gdn_norm_w[li], gdn_w_out[li])
            ssm_p.append(sp.astype(state_ssm.dtype))
            ssm_s.append(ss.astype(state_ssm.dtype))
            conv_p.append(cp)
            conv_s.append(cs)
        xp = xp + yp
        xs = xs + ys
        xp = xp + sqrelu_mlp(rmsnorm(xp, norm_ffn[i]), ffn_w1[i], ffn_w2[i])
        xs = xs + sqrelu_mlp(rmsnorm(xs, norm_ffn[i]), ffn_w1[i], ffn_w2[i])
    y_prompt = rmsnorm(xp, norm_final)
    y_sample = rmsnorm(xs, norm_final)
    return (y_prompt, y_sample, jnp.stack(kv_p), jnp.stack(kv_s), jnp.stack(win_p), jnp.stack(win_s),
            jnp.stack(ssm_p), jnp.stack(ssm_s), jnp.stack(conv_p), jnp.stack(conv_s))
```

```python
import functools
import math

import jax
import jax.numpy as jnp
from jax import lax
from jax.experimental import pallas as pl
from jax.experimental.pallas import tpu as pltpu

F32 = jnp.float32
BF16 = jnp.bfloat16
HIGHEST = lax.Precision.HIGHEST

LANES = 128
VMEM_LIMIT = 56 * 1024 * 1024

NSA_HEAD_DIM = 64
NSA_KV_GROUPS = 4
NSA_N_KV = 6
NSA_CACHED = 4
CMP_BLOCK = 32
CMP_STRIDE = 16
SEL_BLOCK = 64
SEL_TOP_N = 8
WINDOW = 512
FORCE_BONUS = 1e4
REL_BUCKETS = 32
REL_MAX_DIST = 128
GDN_HEAD = 128
GDN_CONV = 4
GDN_CHUNK = 64
RMS_EPS = 1e-6
NEG = -1e30
VIS = -1e29
LOWEST = -3e38

NT = (((1,), (1,)), ((), ()))


def _params(sem):
    return pltpu.CompilerParams(dimension_semantics=sem, vmem_limit_bytes=VMEM_LIMIT)


def _dot(a, b):
    return jnp.dot(a.astype(BF16), b.astype(BF16), preferred_element_type=F32)


def _dot_nt(a, b):
    return lax.dot_general(a.astype(BF16), b.astype(BF16), NT, preferred_element_type=F32)


def _dot_split(a, b):
    ah, bh = a.astype(BF16), b.astype(BF16)
    al = (a - ah.astype(F32)).astype(BF16)
    bl = (b - bh.astype(F32)).astype(BF16)
    hh = jnp.dot(ah, bh, preferred_element_type=F32)
    return hh + (jnp.dot(ah, bl, preferred_element_type=F32) + jnp.dot(al, bh, preferred_element_type=F32))


def _dot_f32(a, b):
    return jnp.dot(a, b, precision=HIGHEST, preferred_element_type=F32)


def _rms(x, gain):
    ms = jnp.mean(x * x, axis=-1, keepdims=True)
    return x * lax.rsqrt(ms + RMS_EPS) * gain


def _norm_matmul_kernel(x_ref, g_ref, w_ref, o_ref, xn_ref):
    @pl.when(pl.program_id(1) == 0)
    def _():
        xn_ref[...] = _rms(x_ref[...], g_ref[...]).astype(BF16)

    o_ref[...] = jnp.dot(xn_ref[...], w_ref[...], preferred_element_type=F32)


def norm_matmul(x, gain, w, *, tm, tn):
    m, d = x.shape
    n = w.shape[1]
    assert m % tm == 0 and n % tn == 0
    return pl.pallas_call(
        _norm_matmul_kernel,
        grid=(m // tm, n // tn),
        in_specs=[
            pl.BlockSpec((tm, d), lambda i, j: (i, 0)),
            pl.BlockSpec((1, d), lambda i, j: (0, 0)),
            pl.BlockSpec((d, tn), lambda i, j: (0, j)),
        ],
        out_specs=pl.BlockSpec((tm, tn), lambda i, j: (i, j)),
        out_shape=jax.ShapeDtypeStruct((m, n), F32),
        scratch_shapes=[pltpu.VMEM((tm, d), BF16)],
        compiler_params=_params(("parallel", "arbitrary")),
        name="norm_matmul",
    )(x, gain.reshape(1, d), w)


def _mix_ffn_kernel(x_ref, o_ref, wo_ref, g_ref, w1_ref, w2_ref, gf_ref, y_ref,
                    x1_ref, xn_ref, acc_ref, *, final_norm):
    j = pl.program_id(1)

    @pl.when(j == 0)
    def _():
        x1 = x_ref[...] + jnp.dot(o_ref[...], wo_ref[...], preferred_element_type=F32)
        x1_ref[...] = x1
        xn_ref[...] = _rms(x1, g_ref[...]).astype(BF16)
        acc_ref[...] = jnp.zeros_like(acc_ref)

    hid = jnp.maximum(jnp.dot(xn_ref[...], w1_ref[...], preferred_element_type=F32), 0.0)
    acc_ref[...] += jnp.dot((hid * hid).astype(BF16), w2_ref[...], preferred_element_type=F32)

    @pl.when(j == pl.num_programs(1) - 1)
    def _():
        y = x1_ref[...] + acc_ref[...]
        if final_norm:
            y = _rms(y, gf_ref[...])
        y_ref[...] = y


def mix_ffn(x, o, w_out, gain, w1, w2, gain_final, *, final_norm, tm, th):
    m, d = x.shape
    ko = o.shape[1]
    hdim = w1.shape[1]
    assert m % tm == 0 and hdim % th == 0
    return pl.pallas_call(
        functools.partial(_mix_ffn_kernel, final_norm=final_norm),
        grid=(m // tm, hdim // th),
        in_specs=[
            pl.BlockSpec((tm, d), lambda i, j: (i, 0)),
            pl.BlockSpec((tm, ko), lambda i, j: (i, 0)),
            pl.BlockSpec((ko, d), lambda i, j: (0, 0)),
            pl.BlockSpec((1, d), lambda i, j: (0, 0)),
            pl.BlockSpec((d, th), lambda i, j: (0, j)),
            pl.BlockSpec((th, d), lambda i, j: (j, 0)),
            pl.BlockSpec((1, d), lambda i, j: (0, 0)),
        ],
        out_specs=pl.BlockSpec((tm, d), lambda i, j: (i, 0)),
        out_shape=jax.ShapeDtypeStruct((m, d), F32),
        scratch_shapes=[pltpu.VMEM((tm, d), F32), pltpu.VMEM((tm, d), BF16), pltpu.VMEM((tm, d), F32)],
        compiler_params=_params(("parallel", "arbitrary")),
        name="mix_ffn",
    )(x, o, w_out, gain.reshape(1, d), w1, w2, gain_final.reshape(1, d))


def _rel_bucket(dist):
    n = jnp.maximum(dist, 0)
    max_exact = REL_BUCKETS // 2
    nf = jnp.maximum(n, 1).astype(F32)
    large = max_exact + (jnp.log(nf / max_exact) / math.log(REL_MAX_DIST / max_exact)
                         * (REL_BUCKETS - max_exact)).astype(jnp.int32)
    return jnp.where(n < max_exact, n, jnp.minimum(large, REL_BUCKETS - 1))


def _dist_bias(rel_bias, dist, visible):
    tbl = rel_bias.astype(F32)
    b = jnp.moveaxis(tbl[_rel_bucket(dist)], -1, 0)
    return jnp.where(visible[None], b, NEG)


def _cmp_sel_overlap(nc, ns):
    c0 = jnp.arange(nc) * CMP_STRIDE
    s0 = jnp.arange(ns) * SEL_BLOCK
    ov = jnp.minimum(c0[:, None] + CMP_BLOCK, s0[None, :] + SEL_BLOCK) - jnp.maximum(c0[:, None], s0[None, :])
    return jnp.maximum(ov, 0).astype(F32) / CMP_BLOCK


def _compress_weights(cw1, cw2, cpe):
    g = NSA_KV_GROUPS
    eye = jnp.eye(g, dtype=F32)
    w1 = jnp.einsum("zide,gh->zigdhe", cw1.astype(F32), eye)
    w1 = w1.reshape(2, CMP_BLOCK, g * NSA_HEAD_DIM, g * NSA_HEAD_DIM).astype(BF16)
    w2 = jnp.einsum("zde,gh->zgdhe", cw2.astype(F32), eye)
    w2 = w2.reshape(2, g * NSA_HEAD_DIM, g * NSA_HEAD_DIM).astype(BF16)
    hid0 = jnp.einsum("zid,zide->ze", cpe.astype(F32), cw1.astype(F32), precision=HIGHEST)
    hid0 = jnp.tile(hid0[:, None, :], (1, 1, g))
    return w1, w2, hid0


def _compress(load_rows, w1_ref, w2_ref, b_ref, z):
    r = CMP_BLOCK // CMP_STRIDE
    assert r == 2
    acc_a = acc_b = None
    for i in range(CMP_STRIDE):
        xi = load_rows(i).astype(BF16)
        da = jnp.dot(xi, w1_ref[z, i], preferred_element_type=F32)
        db = jnp.dot(xi, w1_ref[z, CMP_STRIDE + i], preferred_element_type=F32)
        acc_a = da if acc_a is None else acc_a + da
        acc_b = db if acc_b is None else acc_b + db
    n_str = acc_a.shape[0]
    hid = acc_a + pltpu.roll(acc_b, n_str - 1, axis=0) + b_ref[z]
    act = hid * jax.nn.sigmoid(hid)
    return jnp.dot(act.astype(BF16), w2_ref[z], preferred_element_type=F32)


def _top_k_mask(score, index, axis, k):
    sel = jnp.zeros(score.shape, F32)
    for _ in range(k):
        mx = jnp.max(score, axis=axis, keepdims=True)
        cand = jnp.where(score == mx, index, 1e9)
        first = jnp.min(cand, axis=axis, keepdims=True)
        hit = index == first
        sel = jnp.where(hit, 1.0, sel)
        score = jnp.where(hit, LOWEST, score)
    return sel


def _nsa_compress_kernel(*refs, n_str, n_half):
    g, dh = NSA_KV_GROUPS, NSA_HEAD_DIM
    srcs = (refs[:n_half], refs[n_half:2 * n_half])
    w1_ref, w2_ref, b_ref, kc_ref, vc_ref = refs[2 * n_half:]
    for z, dst in enumerate((kc_ref, vc_ref)):
        def load_rows(i, z=z):
            return jnp.concatenate([r[pl.ds(i, n_str, stride=CMP_STRIDE), :] for r in srcs[z]], axis=1)

        res = _compress(load_rows, w1_ref, w2_ref, b_ref, z)
        for gi in range(g):
            dst[0, gi] = res[:, gi * dh:(gi + 1) * dh].astype(BF16)


def nsa_compress(p, n, t, qw, w1, w2, hid0):
    gd = NSA_KV_GROUPS * NSA_HEAD_DIM
    n_str = t // CMP_STRIDE
    assert qw % LANES == 0 and gd % LANES == 0
    n_half = gd // LANES
    q_blocks = qw // LANES
    out = jax.ShapeDtypeStruct((n, NSA_KV_GROUPS, n_str, NSA_HEAD_DIM), BF16)
    return pl.pallas_call(
        functools.partial(_nsa_compress_kernel, n_str=n_str, n_half=n_half),
        grid=(n,),
        in_specs=[pl.BlockSpec((t, LANES), lambda i, c=c: (i, q_blocks + c)) for c in range(2 * n_half)] + [
            pl.BlockSpec(w1.shape, lambda i: (0, 0, 0, 0)),
            pl.BlockSpec(w2.shape, lambda i: (0, 0, 0)),
            pl.BlockSpec(hid0.shape, lambda i: (0, 0, 0)),
        ],
        out_specs=[pl.BlockSpec((1, NSA_KV_GROUPS, n_str, NSA_HEAD_DIM), lambda i: (i, 0, 0, 0))] * 2,
        out_shape=[out, out],
        compiler_params=_params(("parallel",)),
        name="nsa_compress",
    )(*([p] * (2 * n_half)), w1, w2, hid0)


def _flash_tile(carry, q4, k, v, bias, keep, hpg, tq):
    m, l, acc = carry
    tk = k.shape[0]
    s = _dot_nt(q4, k).reshape(hpg, tq, tk) + bias
    if keep is not None:
        s = jnp.where(keep, s, NEG)
    m_new = jnp.maximum(m, jnp.max(s, axis=-1, keepdims=True))
    a = jnp.exp(m - m_new)
    p = jnp.exp(s - m_new)
    l = a * l + jnp.sum(p, axis=-1, keepdims=True)
    pv = jnp.dot(p.reshape(hpg * tq, tk).astype(BF16), v, preferred_element_type=F32)
    acc = a * acc + pv.reshape(hpg, tq, -1)
    return m_new, l, acc


def _nsa_prompt_kernel(q_ref, gt_ref, kc_ref, vc_ref, ks_ref, vs_ref, kw_ref, vw_ref,
                       cb_ref, bd_ref, ex_ref, ov_ref, o_ref, *, tq, hpg):
    dh = NSA_HEAD_DIM
    qi = pl.program_id(2)
    nsel = ov_ref.shape[1]
    q = q_ref[...] * (dh ** -0.5)
    q4 = jnp.concatenate([q[:, h * dh:(h + 1) * dh] for h in range(hpg)], axis=0).astype(BF16)

    cb = cb_ref[0]
    s = _dot_nt(q4, kc_ref[0, 0]).reshape(hpg, tq, -1) + cb
    m = jnp.max(s, axis=-1, keepdims=True)
    e = jnp.where(cb > VIS, jnp.exp(s - m), 0.0)
    den = jnp.sum(e, axis=-1, keepdims=True)
    pc = e / jnp.where(den > 0, den, 1.0)
    oc = jnp.dot(pc.reshape(hpg * tq, -1).astype(BF16), vc_ref[0, 0], preferred_element_type=F32)
    oc = oc.reshape(hpg, tq, dh)

    imp = _dot_f32(jnp.sum(pc, axis=0), ov_ref[...])
    blk = lax.broadcasted_iota(jnp.int32, (tq, nsel), 1)
    tpos = qi * tq + lax.broadcasted_iota(jnp.int32, (tq, nsel), 0)
    cur = lax.shift_right_logical(tpos, int(math.log2(SEL_BLOCK)))
    forced = (blk == 0) | (blk == cur) | (blk == cur - 1)
    score = jnp.where(blk * SEL_BLOCK <= tpos, imp + FORCE_BONUS * forced.astype(F32), -1e9)
    sel = _top_k_mask(score, blk.astype(F32), 1, min(SEL_TOP_N, nsel)).astype(BF16)

    init = (jnp.full((hpg, tq, 1), NEG, F32), jnp.zeros((hpg, tq, 1), F32), jnp.zeros((hpg, tq, dh), F32))

    def key_tile(ref, j):
        return ref[0, 0, 0, pl.ds(pl.multiple_of(j * tq, tq), tq), :]

    def sel_tile(j, tile, carry):
        keep = jnp.dot(sel, ex_ref[j], preferred_element_type=F32) > 0.5
        return _flash_tile(carry, q4, key_tile(ks_ref, j), key_tile(vs_ref, j), bd_ref[0, tile],
                           keep[None], hpg, tq)

    carry = sel_tile(qi, 0, init)
    carry = lax.fori_loop(0, qi, lambda j, c: sel_tile(j, jnp.where(j == qi - 1, 1, 3), c), carry)
    _, l, acc = carry
    osel = acc / jnp.where(l > 0, l, 1.0)

    carry = init
    n_back = WINDOW // tq
    for dt in range(n_back + 1):
        tile = 0 if dt == 0 else 1 if dt == 1 else 2 if dt == n_back else 3
        j = qi - dt
        jc = jnp.maximum(j, 0)
        bias = bd_ref[0, tile]
        if dt > 0:
            bias = bias + jnp.where(j >= 0, 0.0, NEG)
        carry = _flash_tile(carry, q4, key_tile(kw_ref, jc), key_tile(vw_ref, jc), bias, None, hpg, tq)
    _, l, acc = carry
    owin = acc / jnp.where(l > 0, l, 1.0)

    gt = jax.nn.sigmoid(gt_ref[0])
    outs = []
    for h in range(hpg):
        outs.append(gt[:, 3 * h:3 * h + 1] * oc[h] + gt[:, 3 * h + 1:3 * h + 2] * osel[h]
                    + gt[:, 3 * h + 2:3 * h + 3] * owin[h])
    o_ref[...] = jnp.concatenate(outs, axis=1).astype(BF16)


def nsa_prompt(p, gates, kc, vc, kvt, cmp_bias, diag_bias, expand, ov, n, t, *, tq):
    g, dh = NSA_KV_GROUPS, NSA_HEAD_DIM
    hpg = cmp_bias.shape[1]
    nq = t // tq
    nc = kc.shape[2]
    assert t % tq == 0 and WINDOW % tq == 0 and tq % SEL_BLOCK == 0

    def kv_spec(z):
        return pl.BlockSpec((1, 1, 1, t, dh), lambda b, gi, qi, z=z: (z, b, gi, 0, 0))

    return pl.pallas_call(
        functools.partial(_nsa_prompt_kernel, tq=tq, hpg=hpg),
        grid=(n, g, nq),
        in_specs=[
            pl.BlockSpec((tq, hpg * dh), lambda b, gi, qi: (b * nq + qi, gi)),
            pl.BlockSpec((1, tq, 3 * hpg), lambda b, gi, qi: (gi, b * nq + qi, 0)),
            pl.BlockSpec((1, 1, nc, dh), lambda b, gi, qi: (b, gi, 0, 0)),
            pl.BlockSpec((1, 1, nc, dh), lambda b, gi, qi: (b, gi, 0, 0)),
            kv_spec(0), kv_spec(1), kv_spec(2), kv_spec(3),
            pl.BlockSpec((1, hpg, tq, nc), lambda b, gi, qi: (gi, 0, qi, 0)),
            pl.BlockSpec((1, 4, hpg, tq, tq), lambda b, gi, qi: (gi, 0, 0, 0, 0)),
            pl.BlockSpec(expand.shape, lambda b, gi, qi: (0, 0, 0)),
            pl.BlockSpec(ov.shape, lambda b, gi, qi: (0, 0)),
        ],
        out_specs=pl.BlockSpec((tq, hpg * dh), lambda b, gi, qi: (b * nq + qi, gi)),
        out_shape=jax.ShapeDtypeStruct((n * t, g * hpg * dh), BF16),
        compiler_params=_params(("parallel", "parallel", "arbitrary")),
        name="nsa_prompt",
    )(p, gates, kc, vc, kvt, kvt, kvt, kvt, cmp_bias, diag_bias, expand, ov)


def _nsa_prompt_tables(rel_bias, t, tq, hpg):
    g = NSA_KV_GROUPS
    nc = t // CMP_STRIDE
    ns = -(-t // SEL_BLOCK)
    tpos = jnp.arange(t)
    cmp_end = jnp.arange(nc) * CMP_STRIDE + CMP_BLOCK - 1
    dc = tpos[:, None] - cmp_end[None, :]
    cmp_bias = _dist_bias(rel_bias, dc, (dc >= 0) & (jnp.arange(nc) < nc - 1)[None, :])
    cmp_bias = cmp_bias.reshape(g, hpg, t, nc)
    i = jnp.arange(tq)[:, None]
    j = jnp.arange(tq)[None, :]
    d0 = i - j
    d1 = tq + i - j
    dw = WINDOW + i - j
    far = jnp.full((tq, tq), 2 * REL_MAX_DIST)
    assert tq >= REL_MAX_DIST
    tiles = [
        _dist_bias(rel_bias, d0, d0 >= 0),
        _dist_bias(rel_bias, d1, d1 >= 0),
        _dist_bias(rel_bias, dw, dw < WINDOW),
        _dist_bias(rel_bias, far, far >= 0),
    ]
    diag_bias = jnp.stack(tiles, axis=1).reshape(g, hpg, 4, tq, tq).transpose(0, 2, 1, 3, 4)
    kpos = jnp.arange(ns * SEL_BLOCK)
    expand = (kpos[None, :] // SEL_BLOCK == jnp.arange(ns)[:, None]).astype(BF16)
    expand = expand.reshape(ns, ns * SEL_BLOCK // tq, tq).transpose(1, 0, 2)
    ov = jnp.concatenate([_cmp_sel_overlap(nc - 1, ns), jnp.zeros((1, ns), F32)], axis=0)
    return cmp_bias, diag_bias, expand, ov


def _attend_t(tiles, qbd, s_scr):
    m = None
    for idx, (k, _, add) in enumerate(tiles):
        s = jnp.dot(k.astype(BF16), qbd, preferred_element_type=F32) + add
        s_scr[idx] = s
        mt = jnp.max(s, axis=0, keepdims=True)
        m = mt if m is None else jnp.maximum(m, mt)
    l = acc = None
    for idx, (_, v, _) in enumerate(tiles):
        et = jnp.exp(s_scr[idx] - m).T
        lt = jnp.sum(et, axis=1, keepdims=True)
        pv = jnp.dot(et.astype(BF16), v.astype(BF16), preferred_element_type=F32)
        l = lt if l is None else l + lt
        acc = pv if acc is None else acc + pv
    return acc / jnp.where(l > 0, l, 1.0)


def _nsa_sample_kernel(pt_ref, *refs, n_pages, page, tnew, ns):
    del pt_ref
    pages = refs[:n_pages]
    (ksn_ref, vsn_ref, kwn_ref, vwn_ref, win_ref, qbd_ref, gt_ref, cbt_ref, sbt_ref, wbt_ref,
     ovt_ref, hs_ref, w1_ref, w2_ref, b_ref, o_ref, s_scr) = refs[n_pages:]
    g, dh = NSA_KV_GROUPS, NSA_HEAD_DIM
    gd = g * dh
    rows = qbd_ref.shape[2]
    rpg = rows // g
    qbd = qbd_ref[0]

    cpt = gd // LANES
    pcb = NSA_CACHED * cpt
    wcb = 2 * cpt

    def wide(ref, cbs, tok0, n_tok, tok_stride, cb0):
        return jnp.concatenate(
            [ref[0, 0, pl.ds(tok0 * cbs + cb, n_tok, stride=tok_stride * cbs), :] for cb in range(cb0, cb0 + cpt)],
            axis=1)

    def cached(z):
        return lambda i: jnp.concatenate(
            [wide(pg, pcb, i, page // CMP_STRIDE, CMP_STRIDE, z * cpt) for pg in pages], axis=0)

    kc = _compress(cached(0), w1_ref, w2_ref, b_ref, 0)
    vc = _compress(cached(1), w1_ref, w2_ref, b_ref, 1)

    st = jnp.dot(kc.astype(BF16), qbd, preferred_element_type=F32) + cbt_ref[...]
    m = jnp.max(st, axis=0, keepdims=True)
    e = jnp.where(cbt_ref[...] > VIS, jnp.exp(st - m), 0.0)
    den = jnp.sum(e, axis=0, keepdims=True)
    pct = e / jnp.where(den > 0, den, 1.0)
    oc = jnp.dot(pct.T.astype(BF16), vc.astype(BF16), preferred_element_type=F32)

    imp = _dot_f32(_dot_f32(ovt_ref[...], pct), hs_ref[...])
    nsp = imp.shape[0]
    blk = lax.broadcasted_iota(jnp.int32, (nsp, rows), 0)
    cur = ns - 1
    forced = (blk == 0) | (blk == cur) | (blk == cur - 1)
    score = jnp.where(blk < ns, imp + FORCE_BONUS * forced.astype(F32), LOWEST)
    sel = _top_k_mask(score, blk.astype(F32), 0, min(SEL_TOP_N, ns))
    hide = jnp.where(sel > 0.5, 0.0, NEG)

    def block_rows(b):
        return jnp.broadcast_to(hide[b:b + 1], (SEL_BLOCK, rows))

    pad = jnp.zeros((page - tnew, gd), F32)
    per_page = page // SEL_BLOCK
    tiles = []
    for k, pg in enumerate(pages):
        add = sbt_ref[k] + jnp.concatenate([block_rows(per_page * k + b) for b in range(per_page)], axis=0)
        tiles.append((wide(pg, pcb, 0, page, 1, 2 * cpt), wide(pg, pcb, 0, page, 1, 3 * cpt), add))
    add = sbt_ref[n_pages] + jnp.concatenate(
        [block_rows(min(per_page * n_pages + b, nsp - 1)) for b in range(per_page)], axis=0)
    tiles.append((jnp.concatenate([ksn_ref[...], pad], axis=0),
                  jnp.concatenate([vsn_ref[...], pad], axis=0), add))
    osel = _attend_t(tiles, qbd, s_scr)

    wbuf = win_ref.shape[2] // wcb
    tiles = []
    for k in range(wbuf // page):
        tiles.append((wide(win_ref, wcb, k * page, page, 1, 0), wide(win_ref, wcb, k * page, page, 1, cpt),
                      wbt_ref[k]))
    tiles.append((jnp.concatenate([kwn_ref[...], pad], axis=0),
                  jnp.concatenate([vwn_ref[...], pad], axis=0), wbt_ref[wbuf // page]))
    owin = _attend_t(tiles, qbd, s_scr)

    gt = jax.nn.sigmoid(gt_ref[0])
    o = gt[:, 0:1] * oc + gt[:, 1:2] * osel + gt[:, 2:3] * owin
    for gi in range(g):
        o_ref[0, gi] = o[gi * rpg:(gi + 1) * rpg, gi * dh:(gi + 1) * dh].astype(BF16)


def nsa_sample(p, row0, cache_kv, layer, cache_win, page_table, qbd, gates, tables, cweights, *, tnew):
    n, n_pages = page_table.shape
    g, dh = NSA_KV_GROUPS, NSA_HEAD_DIM
    gd = g * dh
    page = cache_kv.shape[2]
    rows = qbd.shape[2]
    cbt, sbt, wbt, ovt, hs = tables
    w1, w2, hid0 = cweights
    ns = -(-(n_pages * page + tnew) // SEL_BLOCK)
    assert row0 % tnew == 0 and page % SEL_BLOCK == 0 and page == LANES and tnew <= page
    assert (n_pages * page + tnew) // CMP_STRIDE == n_pages * page // CMP_STRIDE
    blk0 = row0 // tnew
    q_blocks = (rows // tnew * dh) // gd
    assert gd % LANES == 0
    prow = page * NSA_CACHED * gd // LANES
    cache = cache_kv.reshape(cache_kv.shape[0], cache_kv.shape[1], prow, LANES)
    wbuf = cache_win.shape[2]
    wrow = wbuf * 2 * gd // LANES
    win = cache_win.reshape(cache_win.shape[0], n, wrow, LANES)
    assert wbuf % page == 0

    def page_spec(k):
        return pl.BlockSpec((1, 1, prow, LANES), lambda i, pt, k=k: (layer, pt[i * n_pages + k], 0, 0))

    def new_spec(z):
        return pl.BlockSpec((tnew, gd), lambda i, pt, z=z: (blk0 + i, q_blocks + z))

    def const(shape):
        return pl.BlockSpec(shape, lambda i, pt: (0,) * len(shape))

    grid_spec = pltpu.PrefetchScalarGridSpec(
        num_scalar_prefetch=1,
        grid=(n,),
        in_specs=[page_spec(k) for k in range(n_pages)] + [
            new_spec(2), new_spec(3), new_spec(4), new_spec(5),
            pl.BlockSpec((1, 1, wrow, LANES), lambda i, pt: (layer, i, 0, 0)),
            pl.BlockSpec((1, gd, rows), lambda i, pt: (i, 0, 0)),
            pl.BlockSpec((1, rows, 3), lambda i, pt: (i, 0, 0)),
            const(cbt.shape), const(sbt.shape), const(wbt.shape), const(ovt.shape), const(hs.shape),
            const(w1.shape), const(w2.shape), const(hid0.shape),
        ],
        out_specs=pl.BlockSpec((1, g, rows // g, dh), lambda i, pt: (i, 0, 0, 0)),
        scratch_shapes=[pltpu.VMEM((n_pages + 1, page, rows), F32)],
    )
    return pl.pallas_call(
        functools.partial(_nsa_sample_kernel, n_pages=n_pages, page=page, tnew=tnew, ns=ns),
        grid_spec=grid_spec,
        out_shape=jax.ShapeDtypeStruct((n, g, rows // g, dh), BF16),
        compiler_params=_params(("arbitrary",)),
        name="nsa_sample",
    )(page_table.reshape(-1), *([cache] * n_pages), p, p, p, p, win, qbd, gates,
      cbt, sbt, wbt, ovt, hs, w1, w2, hid0)


def _nsa_sample_tables(rel_bias, past, tnew, wbuf, page, hpg):
    g = NSA_KV_GROUPS
    h = g * hpg
    qpos = past + jnp.arange(tnew)
    n_str = (past + tnew) // CMP_STRIDE
    ns = -(-(past + tnew) // SEL_BLOCK)
    nsp = -(-ns // 8) * 8

    def lanes(b):
        return b.transpose(1, 0, 2).reshape(b.shape[1], h * tnew)

    cmp_end = jnp.arange(n_str) * CMP_STRIDE + CMP_BLOCK - 1
    dc = qpos[None, :] - cmp_end[:, None]
    cbt = lanes(_dist_bias(rel_bias, dc, (dc >= 0) & (jnp.arange(n_str) < n_str - 1)[:, None]))
    n_tiles = past // page + 1
    kpos = jnp.arange(n_tiles * page)
    ds = qpos[None, :] - kpos[:, None]
    sbt = lanes(_dist_bias(rel_bias, ds, (ds >= 0) & (kpos < past + tnew)[:, None]))
    sbt = sbt.reshape(n_tiles, page, h * tnew)
    w_tiles = wbuf // page + 1
    r = jnp.arange(w_tiles * page)
    dw = qpos[None, :] - (past - wbuf + r)[:, None]
    wbt = lanes(_dist_bias(rel_bias, dw, (dw >= 0) & (dw < WINDOW) & (r < wbuf + tnew)[:, None]))
    wbt = wbt.reshape(w_tiles, page, h * tnew)
    ov = _cmp_sel_overlap(n_str - 1, ns)
    ovt = jnp.zeros((nsp, n_str), F32).at[:ns, :n_str - 1].set(ov.T)
    lane = jnp.arange(h * tnew)
    same = (lane[:, None] // (hpg * tnew) == lane[None, :] // (hpg * tnew)) & \
           (lane[:, None] % tnew == lane[None, :] % tnew)
    return cbt, sbt, wbt, ovt, same.astype(F32)


def _softplus(x):
    return jnp.maximum(x, 0.0) + jnp.log1p(jnp.exp(-jnp.abs(x)))


def _unit_lower_inverse(a):
    c = a.shape[0]
    eye = (lax.broadcasted_iota(jnp.int32, (c, c), 0) == lax.broadcasted_iota(jnp.int32, (c, c), 1)).astype(F32)
    inv = eye - a
    power = a
    n = 2
    while n < c:
        power = _dot(power, power)
        inv = inv + _dot(inv, power)
        n *= 2
    resid = (eye - inv) - _dot_split(a, inv)
    return inv + _dot(inv, resid)


def _gdn_kernel(qkv_ref, z_ref, ab_ref, hist_ref, s0_ref, cw_ref, nar_ref, dtr_ref, nac_ref, dtc_ref,
                nw_ref, o_ref, sfin_ref, s_scr, carry_scr, *, c, tv, qk_heads, v_heads):
    hd = GDN_HEAD
    halo = carry_scr.shape[0]
    step = pl.program_id(1)

    @pl.when(step == 0)
    def _():
        s_scr[...] = s0_ref[0]
        carry_scr[...] = hist_ref[0]

    x = qkv_ref[...]
    ext = jnp.concatenate([carry_scr[...], x], axis=0)
    off = halo - (GDN_CONV - 1)
    conv = ext[off:off + tv] * cw_ref[0:1]
    for i in range(1, GDN_CONV):
        conv = conv + ext[off + i:off + i + tv] * cw_ref[i:i + 1]
    carry_scr[...] = ext[tv:tv + halo]
    act = conv * jax.nn.sigmoid(conv)
    if tv < c:
        act = jnp.concatenate([act, jnp.zeros((c - tv, act.shape[1]), F32)], axis=0)
        ab = jnp.concatenate([ab_ref[...], jnp.zeros((c - tv, ab_ref.shape[1]), F32)], axis=0)
    else:
        ab = ab_ref[...]

    row = lax.broadcasted_iota(jnp.int32, (c, c), 0)
    col = lax.broadcasted_iota(jnp.int32, (c, c), 1)
    incl = row >= col
    strict = row > col
    lower = incl.astype(F32)
    upper = (row <= col).astype(F32)

    live_c = lax.broadcasted_iota(jnp.int32, ab.shape, 0) < tv
    g_col = jnp.where(live_c, nar_ref[...] * _softplus(ab + dtr_ref[...]), 0.0)
    beta = jnp.where(live_c, jax.nn.sigmoid(ab), 0.0)
    abt = ab.T[0:v_heads]
    live_r = lax.broadcasted_iota(jnp.int32, abt.shape, 1) < tv
    g_row = jnp.where(live_r, nac_ref[...] * _softplus(abt + dtc_ref[...]), 0.0)
    gam_col = _dot_f32(lower, g_col)
    gam_row = _dot_f32(g_row, upper)

    rep = v_heads // qk_heads
    for hq in range(qk_heads):
        qh = act[:, hq * hd:(hq + 1) * hd]
        kh = act[:, (qk_heads + hq) * hd:(qk_heads + hq + 1) * hd]
        qh = qh * lax.rsqrt(jnp.sum(qh * qh, axis=-1, keepdims=True) + 1e-6) * (hd ** -0.5)
        kh = kh * lax.rsqrt(jnp.sum(kh * kh, axis=-1, keepdims=True) + 1e-6)
        kk = _dot_nt(kh, kh)
        qk = _dot_nt(qh, kh)
        for r in range(rep):
            hv = hq * rep + r
            vh = act[:, (2 * qk_heads + hv) * hd:(2 * qk_heads + hv + 1) * hd]
            gc = gam_col[:, hv:hv + 1]
            gr = gam_row[hv:hv + 1, :]
            bc = beta[:, v_heads + hv:v_heads + hv + 1]
            decay = jnp.where(incl, jnp.exp(jnp.where(incl, gc - gr, 0.0)), 0.0)
            a_mat = jnp.where(strict, bc * kk * decay, 0.0)
            tinv = _unit_lower_inverse(a_mat)
            eg = jnp.exp(gc)
            g_last = gc[c - 1:c, :]
            value = _dot(tinv, vh * bc)
            kcum = _dot(tinv, kh * (bc * eg))
            s_old = s_scr[hv]
            u = value - _dot(kcum, s_old)
            o = _dot(qh * eg, s_old) + _dot(qk * decay, u)
            k_dec = kh * jnp.exp(g_last - gc)
            s_scr[hv] = s_old * jnp.exp(g_last) + _dot(k_dec.T, u)
            o = o[0:tv]
            zh = z_ref[:, hv * hd:(hv + 1) * hd]
            o = _rms(o, nw_ref[...]) * (zh * jax.nn.sigmoid(zh))
            o_ref[:, hv * hd:(hv + 1) * hd] = o.astype(BF16)

    @pl.when(step == pl.num_programs(1) - 1)
    def _():
        sfin_ref[0] = s_scr[...]


def gdn(p, row0, n, t, hist, s0, conv_w, a_log, dt_bias, norm_w, *, qk_heads, v_heads):
    hd = GDN_HEAD
    ch = (2 * qk_heads + v_heads) * hd
    vw = v_heads * hd
    c = GDN_CHUNK
    tv = min(c, t)
    assert t % tv == 0 and row0 % tv == 0 and ch % vw == 0
    nchunk = t // tv
    blk0 = row0 // tv
    halo = 8
    histp = jnp.concatenate([jnp.zeros((n, halo - (GDN_CONV - 1), ch), F32), hist.astype(F32)], axis=1)
    cw = jnp.concatenate([conv_w.astype(F32), jnp.zeros((halo - GDN_CONV, ch), F32)], axis=0)
    neg_a = -jnp.exp(a_log.astype(F32))
    dt = dt_bias.astype(F32)
    pad = jnp.zeros((LANES - v_heads,), F32)
    nar = jnp.concatenate([neg_a, pad]).reshape(1, LANES)
    dtr = jnp.concatenate([dt, pad]).reshape(1, LANES)
    o, sfin = pl.pallas_call(
        functools.partial(_gdn_kernel, c=c, tv=tv, qk_heads=qk_heads, v_heads=v_heads),
        grid=(n, nchunk),
        in_specs=[
            pl.BlockSpec((tv, ch), lambda b, s: (blk0 + b * nchunk + s, 0)),
            pl.BlockSpec((tv, vw), lambda b, s: (blk0 + b * nchunk + s, ch // vw)),
            pl.BlockSpec((tv, LANES), lambda b, s: (blk0 + b * nchunk + s, (ch + vw) // LANES)),
            pl.BlockSpec((1, halo, ch), lambda b, s: (b, 0, 0)),
            pl.BlockSpec((1, v_heads, hd, hd), lambda b, s: (b, 0, 0, 0)),
            pl.BlockSpec((halo, ch), lambda b, s: (0, 0)),
            pl.BlockSpec((1, LANES), lambda b, s: (0, 0)),
            pl.BlockSpec((1, LANES), lambda b, s: (0, 0)),
            pl.BlockSpec((v_heads, 1), lambda b, s: (0, 0)),
            pl.BlockSpec((v_heads, 1), lambda b, s: (0, 0)),
            pl.BlockSpec((1, hd), lambda b, s: (0, 0)),
        ],
        out_specs=[
            pl.BlockSpec((tv, vw), lambda b, s: (b * nchunk + s, 0)),
            pl.BlockSpec((1, v_heads, hd, hd), lambda b, s: (b, 0, 0, 0)),
        ],
        out_shape=[jax.ShapeDtypeStruct((n * t, vw), BF16),
                   jax.ShapeDtypeStruct((n, v_heads, hd, hd), F32)],
        scratch_shapes=[pltpu.VMEM((v_heads, hd, hd), F32), pltpu.VMEM((halo, ch), F32)],
        compiler_params=_params(("parallel", "arbitrary")),
        name="gdn",
    )(p, p, p, histp, s0.astype(F32), cw, nar, dtr, neg_a.reshape(v_heads, 1), dt.reshape(v_heads, 1),
      norm_w.astype(F32).reshape(1, hd))
    return o, sfin


def _pow2_divisor(m, cap):
    tile = cap
    while m % tile:
        tile //= 2
    return tile


def _pad_cols(w, mult):
    n = w.shape[-1]
    return jnp.pad(w, ((0, 0), (0, -n % mult)))


def kernel(x_prompt, x_sample, cache_kv, cache_win, state_ssm, state_conv, page_table, rel_bias, norm_mix, norm_ffn, norm_final, nsa_w_in, nsa_w_out, nsa_cmp_w1, nsa_cmp_w2, nsa_cmp_pe, gdn_w_in, gdn_conv_w, gdn_a_log, gdn_dt_bias, gdn_norm_w, gdn_w_out, ffn_w1, ffn_w2):
    nb, t, d = x_prompt.shape
    ns_, tnew, _ = x_sample.shape
    depth = norm_mix.shape[0]
    g, dh = NSA_KV_GROUPS, NSA_HEAD_DIM
    gd = g * dh
    heads = nsa_w_out.shape[1] // dh
    hpg = heads // g
    qw = heads * dh
    mp = nb * t
    past = page_table.shape[1] * cache_kv.shape[2]
    wbuf = cache_win.shape[2]
    page = cache_kv.shape[2]
    v_heads = gdn_a_log.shape[1]
    qk_heads = (gdn_conv_w.shape[2] // GDN_HEAD - v_heads) // 2
    conv_ch = gdn_conv_w.shape[2]
    tn = 7 * LANES
    m_all = nb * t + ns_ * tnew
    tm = _pow2_divisor(m_all, 1024)
    tm_ffn = _pow2_divisor(m_all, 512)

    x = jnp.concatenate([x_prompt.reshape(mp, d), x_sample.reshape(ns_ * tnew, d)], axis=0).astype(F32)
    tq = LANES
    p_tables = _nsa_prompt_tables(rel_bias, t, tq, hpg)
    s_tables = _nsa_sample_tables(rel_bias, past, tnew, wbuf, page, hpg)

    kv_p, kv_s, win_p, win_s, ssm_p, ssm_s, conv_p, conv_s = [], [], [], [], [], [], [], []
    for i in range(depth):
        li = i // 2
        if i % 2 == 0:
            w_in = _pad_cols(nsa_w_in[li], tn).astype(BF16)
            p = norm_matmul(x, norm_mix[i], w_in, tm=tm, tn=tn)
            pp = p[:mp].reshape(nb, t, -1)
            ps = p[mp:].reshape(ns_, tnew, -1)
            kv_end = qw + NSA_N_KV * gd
            cweights = _compress_weights(nsa_cmp_w1[li], nsa_cmp_w2[li], nsa_cmp_pe[li])
            kc, vc = nsa_compress(p, nb, t, qw, *cweights)
            kvt = pp[:, :, qw + 2 * gd:kv_end].reshape(nb, t, 4, g, dh).transpose(2, 0, 3, 1, 4).astype(BF16)
            gates_p = pp[:, :, kv_end:kv_end + 3 * heads].reshape(mp, g, 3 * hpg).transpose(1, 0, 2)
            o_p = nsa_prompt(p, gates_p, kc, vc, kvt, *p_tables, nb, t, tq=tq)
            q_s = (ps[:, :, :qw] * dh ** -0.5).reshape(ns_, tnew, g, hpg, dh)
            qbd = jnp.einsum("ntghd,gk->ngdkht", q_s, jnp.eye(g, dtype=F32))
            qbd = qbd.reshape(ns_, gd, heads * tnew).astype(BF16)
            gates_s = ps[:, :, kv_end:kv_end + 3 * heads].reshape(ns_, tnew, heads, 3)
            gates_s = gates_s.transpose(0, 2, 1, 3).reshape(ns_, heads * tnew, 3)
            o_s = nsa_sample(p, mp, cache_kv, li, cache_win, page_table, qbd, gates_s, s_tables, cweights,
                             tnew=tnew)
            o_s = o_s.reshape(ns_, g, hpg, tnew, dh).transpose(0, 3, 1, 2, 4).reshape(ns_ * tnew, qw)
            o = jnp.concatenate([o_p, o_s], axis=0)
            w_out = nsa_w_out[li].astype(BF16)
            kv_p.append(pp[:, :, qw:qw + NSA_CACHED * gd].reshape(nb, t, NSA_CACHED, g, dh))
            kv_s.append(ps[:, :, qw:qw + NSA_CACHED * gd].reshape(ns_, tnew, NSA_CACHED, g, dh))
            wkeep = min(WINDOW, t)
            win_p.append(pp[:, t - wkeep:, qw + NSA_CACHED * gd:kv_end].reshape(nb, wkeep, 2, g, dh))
            new_win = ps[:, :, qw + NSA_CACHED * gd:kv_end].reshape(ns_, tnew, 2, g, dh)
            win_s.append(jnp.concatenate([cache_win[li], new_win.astype(cache_win.dtype)], axis=1)[:, -wbuf:])
        else:
            w_in = _pad_cols(gdn_w_in[li], tn).astype(BF16)
            p = norm_matmul(x, norm_mix[i], w_in, tm=tm, tn=tn)
            pp = p[:mp].reshape(nb, t, -1)
            ps = p[mp:].reshape(ns_, tnew, -1)
            args = (gdn_conv_w[li], gdn_a_log[li], gdn_dt_bias[li], gdn_norm_w[li])
            h0 = jnp.zeros((nb, GDN_CONV - 1, conv_ch), F32)
            s0 = jnp.zeros((nb, v_heads, GDN_HEAD, GDN_HEAD), F32)
            o_p, sp = gdn(p, 0, nb, t, h0, s0, *args, qk_heads=qk_heads, v_heads=v_heads)
            o_s, ss = gdn(p, mp, ns_, tnew, state_conv[li], state_ssm[li], *args,
                          qk_heads=qk_heads, v_heads=v_heads)
            o = jnp.concatenate([o_p, o_s], axis=0)
            w_out = gdn_w_out[li].astype(BF16)
            ssm_p.append(sp.astype(state_ssm.dtype))
            ssm_s.append(ss.astype(state_ssm.dtype))
            keep = GDN_CONV - 1
            conv_p.append(jnp.concatenate([h0, pp[:, :, :conv_ch]], axis=1)[:, -keep:])
            conv_s.append(jnp.concatenate([state_conv[li].astype(F32), ps[:, :, :conv_ch]], axis=1)[:, -keep:])
        x = mix_ffn(x, o, w_out, norm_ffn[i], ffn_w1[i].astype(BF16), ffn_w2[i].astype(BF16), norm_final,
                    final_norm=(i == depth - 1), tm=tm_ffn, th=512)
    y_prompt = x[:mp].reshape(nb, t, d)
    y_sample = x[mp:].reshape(ns_, tnew, d)
    return (y_prompt, y_sample, jnp.stack(kv_p), jnp.stack(kv_s), jnp.stack(win_p), jnp.stack(win_s),
            jnp.stack(ssm_p), jnp.stack(ssm_s), jnp.stack(conv_p), jnp.stack(conv_s))
```

```python
import functools
import math

import jax
import jax.numpy as jnp
from jax import lax
from jax.experimental import pallas as pl
from jax.experimental.pallas import tpu as pltpu

F32 = jnp.float32
BF16 = jnp.bfloat16
HIGHEST = lax.Precision.HIGHEST

LANES = 128
VMEM_LIMIT = 56 * 1024 * 1024

NSA_HEAD_DIM = 64
NSA_KV_GROUPS = 4
NSA_N_KV = 6
NSA_CACHED = 4
CMP_BLOCK = 32
CMP_STRIDE = 16
SEL_BLOCK = 64
SEL_TOP_N = 8
WINDOW = 512
FORCE_BONUS = 1e4
REL_BUCKETS = 32
REL_MAX_DIST = 128
GDN_HEAD = 128
GDN_CONV = 4
GDN_CHUNK = 64
RMS_EPS = 1e-6
NEG = -1e30
VIS = -1e29
LOWEST = -3e38

NT = (((1,), (1,)), ((), ()))


def _params(sem):
    return pltpu.CompilerParams(dimension_semantics=sem, vmem_limit_bytes=VMEM_LIMIT)


def _dot(a, b):
    return jnp.dot(a.astype(BF16), b.astype(BF16), preferred_element_type=F32)


def _dot_nt(a, b):
    return lax.dot_general(a.astype(BF16), b.astype(BF16), NT, preferred_element_type=F32)


def _dot_split(a, b):
    ah, bh = a.astype(BF16), b.astype(BF16)
    al = (a - ah.astype(F32)).astype(BF16)
    bl = (b - bh.astype(F32)).astype(BF16)
    hh = jnp.dot(ah, bh, preferred_element_type=F32)
    return hh + (jnp.dot(ah, bl, preferred_element_type=F32) + jnp.dot(al, bh, preferred_element_type=F32))


def _dot_f32(a, b):
    return jnp.dot(a, b, precision=HIGHEST, preferred_element_type=F32)


def _rms(x, gain):
    ms = jnp.mean(x * x, axis=-1, keepdims=True)
    return x * lax.rsqrt(ms + RMS_EPS) * gain


def _norm_matmul_kernel(x_ref, g_ref, w_ref, o_ref, xn_ref):
    @pl.when(pl.program_id(1) == 0)
    def _():
        xn_ref[...] = _rms(x_ref[...], g_ref[...]).astype(BF16)

    o_ref[...] = jnp.dot(xn_ref[...], w_ref[...], preferred_element_type=F32)


def norm_matmul(x, gain, w, *, tm, tn):
    m, d = x.shape
    n = w.shape[1]
    assert m % tm == 0 and n % tn == 0
    return pl.pallas_call(
        _norm_matmul_kernel,
        grid=(m // tm, n // tn),
        in_specs=[
            pl.BlockSpec((tm, d), lambda i, j: (i, 0)),
            pl.BlockSpec((1, d), lambda i, j: (0, 0)),
            pl.BlockSpec((d, tn), lambda i, j: (0, j)),
        ],
        out_specs=pl.BlockSpec((tm, tn), lambda i, j: (i, j)),
        out_shape=jax.ShapeDtypeStruct((m, n), F32),
        scratch_shapes=[pltpu.VMEM((tm, d), BF16)],
        compiler_params=_params(("parallel", "arbitrary")),
        name="norm_matmul",
    )(x, gain.reshape(1, d), w)


def _mix_ffn_kernel(x_ref, o_ref, wo_ref, g_ref, w1_ref, w2_ref, gf_ref, y_ref,
                    x1_ref, xn_ref, acc_ref, *, final_norm):
    j = pl.program_id(1)

    @pl.when(j == 0)
    def _():
        x1 = x_ref[...] + jnp.dot(o_ref[...], wo_ref[...], preferred_element_type=F32)
        x1_ref[...] = x1
        xn_ref[...] = _rms(x1, g_ref[...]).astype(BF16)
        acc_ref[...] = jnp.zeros_like(acc_ref)

    hid = jnp.maximum(jnp.dot(xn_ref[...], w1_ref[...], preferred_element_type=F32), 0.0)
    acc_ref[...] += jnp.dot((hid * hid).astype(BF16), w2_ref[...], preferred_element_type=F32)

    @pl.when(j == pl.num_programs(1) - 1)
    def _():
        y = x1_ref[...] + acc_ref[...]
        if final_norm:
            y = _rms(y, gf_ref[...])
        y_ref[...] = y


def mix_ffn(x, o, w_out, gain, w1, w2, gain_final, *, final_norm, tm, th):
    m, d = x.shape
    ko = o.shape[1]
    hdim = w1.shape[1]
    assert m % tm == 0 and hdim % th == 0
    return pl.pallas_call(
        functools.partial(_mix_ffn_kernel, final_norm=final_norm),
        grid=(m // tm, hdim // th),
        in_specs=[
            pl.BlockSpec((tm, d), lambda i, j: (i, 0)),
            pl.BlockSpec((tm, ko), lambda i, j: (i, 0)),
            pl.BlockSpec((ko, d), lambda i, j: (0, 0)),
            pl.BlockSpec((1, d), lambda i, j: (0, 0)),
            pl.BlockSpec((d, th), lambda i, j: (0, j)),
            pl.BlockSpec((th, d), lambda i, j: (j, 0)),
            pl.BlockSpec((1, d), lambda i, j: (0, 0)),
        ],
        out_specs=pl.BlockSpec((tm, d), lambda i, j: (i, 0)),
        out_shape=jax.ShapeDtypeStruct((m, d), F32),
        scratch_shapes=[pltpu.VMEM((tm, d), F32), pltpu.VMEM((tm, d), BF16), pltpu.VMEM((tm, d), F32)],
        compiler_params=_params(("parallel", "arbitrary")),
        name="mix_ffn",
    )(x, o, w_out, gain.reshape(1, d), w1, w2, gain_final.reshape(1, d))


def _rel_bucket(dist):
    n = jnp.maximum(dist, 0)
    max_exact = REL_BUCKETS // 2
    nf = jnp.maximum(n, 1).astype(F32)
    large = max_exact + (jnp.log(nf / max_exact) / math.log(REL_MAX_DIST / max_exact)
                         * (REL_BUCKETS - max_exact)).astype(jnp.int32)
    return jnp.where(n < max_exact, n, jnp.minimum(large, REL_BUCKETS - 1))


def _dist_bias(rel_bias, dist, visible):
    tbl = rel_bias.astype(F32)
    b = jnp.moveaxis(tbl[_rel_bucket(dist)], -1, 0)
    return jnp.where(visible[None], b, NEG)


def _cmp_sel_overlap(nc, ns):
    c0 = jnp.arange(nc) * CMP_STRIDE
    s0 = jnp.arange(ns) * SEL_BLOCK
    ov = jnp.minimum(c0[:, None] + CMP_BLOCK, s0[None, :] + SEL_BLOCK) - jnp.maximum(c0[:, None], s0[None, :])
    return jnp.maximum(ov, 0).astype(F32) / CMP_BLOCK


def _compress_weights(cw1, cw2, cpe):
    g = NSA_KV_GROUPS
    eye = jnp.eye(g, dtype=F32)
    w1 = jnp.einsum("zide,gh->zigdhe", cw1.astype(F32), eye)
    w1 = w1.reshape(2, CMP_BLOCK, g * NSA_HEAD_DIM, g * NSA_HEAD_DIM).astype(BF16)
    w2 = jnp.einsum("zde,gh->zgdhe", cw2.astype(F32), eye)
    w2 = w2.reshape(2, g * NSA_HEAD_DIM, g * NSA_HEAD_DIM).astype(BF16)
    hid0 = jnp.einsum("zid,zide->ze", cpe.astype(F32), cw1.astype(F32), precision=HIGHEST)
    hid0 = jnp.tile(hid0[:, None, :], (1, 1, g))
    return w1, w2, hid0


def _compress(load_rows, w1_ref, w2_ref, b_ref, z):
    r = CMP_BLOCK // CMP_STRIDE
    assert r == 2
    acc_a = acc_b = None
    for i in range(CMP_STRIDE):
        xi = load_rows(i).astype(BF16)
        da = jnp.dot(xi, w1_ref[z, i], preferred_element_type=F32)
        db = jnp.dot(xi, w1_ref[z, CMP_STRIDE + i], preferred_element_type=F32)
        acc_a = da if acc_a is None else acc_a + da
        acc_b = db if acc_b is None else acc_b + db
    n_str = acc_a.shape[0]
    hid = acc_a + pltpu.roll(acc_b, n_str - 1, axis=0) + b_ref[z]
    act = hid * jax.nn.sigmoid(hid)
    return jnp.dot(act.astype(BF16), w2_ref[z], preferred_element_type=F32)


def _top_k_mask(score, index, axis, k):
    sel = jnp.zeros(score.shape, F32)
    for _ in range(k):
        mx = jnp.max(score, axis=axis, keepdims=True)
        cand = jnp.where(score == mx, index, 1e9)
        first = jnp.min(cand, axis=axis, keepdims=True)
        hit = index == first
        sel = jnp.where(hit, 1.0, sel)
        score = jnp.where(hit, LOWEST, score)
    return sel


def _nsa_compress_kernel(*refs, n_str, n_half):
    g, dh = NSA_KV_GROUPS, NSA_HEAD_DIM
    srcs = (refs[:n_half], refs[n_half:2 * n_half])
    w1_ref, w2_ref, b_ref, kc_ref, vc_ref = refs[2 * n_half:]
    for z, dst in enumerate((kc_ref, vc_ref)):
        def load_rows(i, z=z):
            return jnp.concatenate([r[pl.ds(i, n_str, stride=CMP_STRIDE), :] for r in srcs[z]], axis=1)

        res = _compress(load_rows, w1_ref, w2_ref, b_ref, z)
        if z == 1:
            res = res.T
        for gi in range(g):
            dst[0, gi] = (res[gi * dh:(gi + 1) * dh] if z == 1 else res[:, gi * dh:(gi + 1) * dh]).astype(BF16)


def nsa_compress(p, n, t, qw, w1, w2, hid0):
    gd = NSA_KV_GROUPS * NSA_HEAD_DIM
    n_str = t // CMP_STRIDE
    assert qw % LANES == 0 and gd % LANES == 0
    n_half = gd // LANES
    q_blocks = qw // LANES
    out = [jax.ShapeDtypeStruct((n, NSA_KV_GROUPS, n_str, NSA_HEAD_DIM), BF16),
           jax.ShapeDtypeStruct((n, NSA_KV_GROUPS, NSA_HEAD_DIM, n_str), BF16)]
    return pl.pallas_call(
        functools.partial(_nsa_compress_kernel, n_str=n_str, n_half=n_half),
        grid=(n,),
        in_specs=[pl.BlockSpec((t, LANES), lambda i, c=c: (i, q_blocks + c)) for c in range(2 * n_half)] + [
            pl.BlockSpec(w1.shape, lambda i: (0, 0, 0, 0)),
            pl.BlockSpec(w2.shape, lambda i: (0, 0, 0)),
            pl.BlockSpec(hid0.shape, lambda i: (0, 0, 0)),
        ],
        out_specs=[pl.BlockSpec((1, NSA_KV_GROUPS, n_str, NSA_HEAD_DIM), lambda i: (i, 0, 0, 0)),
                   pl.BlockSpec((1, NSA_KV_GROUPS, NSA_HEAD_DIM, n_str), lambda i: (i, 0, 0, 0))],
        out_shape=out,
        compiler_params=_params(("parallel",)),
        name="nsa_compress",
    )(*([p] * (2 * n_half)), w1, w2, hid0)


V_AUG = 16


def _softmax_tile_t(carry, s, vt):
    m, acc = carry
    m_new = jnp.maximum(m, jnp.max(s, axis=0, keepdims=True))
    p = jnp.exp(s - m_new).astype(BF16)
    acc = jnp.exp(m - m_new) * acc + jnp.dot(vt, p, preferred_element_type=F32)
    return m_new, acc


def _nsa_prompt_t_kernel(q_ref, gt_ref, kc_ref, vct_ref, ks_ref, vst_ref, kw_ref, vwt_ref,
                         cb_ref, bd_ref, ovt_ref, o_ref, hide_scr, *, tq, hpg, sub):
    dh = NSA_HEAD_DIM
    qi = pl.program_id(2)
    nsel = ovt_ref.shape[0]
    lanes = hpg * tq
    q = q_ref[...] * (dh ** -0.5)
    q4 = jnp.concatenate([q[:, h * dh:(h + 1) * dh] for h in range(hpg)], axis=0).astype(BF16)

    def logits(k):
        return lax.dot_general(k, q4, NT, preferred_element_type=F32)

    cb = cb_ref[0, 0]
    s = logits(kc_ref[0, 0]) + cb
    m = jnp.max(s, axis=0, keepdims=True)
    e = jnp.where(cb > VIS, jnp.exp(s - m), 0.0)
    den = jnp.sum(e, axis=0, keepdims=True)
    pc = e / jnp.where(den > 0, den, 1.0)
    oc = jnp.dot(vct_ref[0, 0], pc.astype(BF16), preferred_element_type=F32)

    pcs = pc[:, 0:tq]
    for h in range(1, hpg):
        pcs = pcs + pc[:, h * tq:(h + 1) * tq]
    imp = _dot_f32(ovt_ref[...], pcs)
    blk = lax.broadcasted_iota(jnp.int32, (nsel, tq), 0)
    tpos = qi * tq + lax.broadcasted_iota(jnp.int32, (nsel, tq), 1)
    cur = lax.shift_right_logical(tpos, int(math.log2(SEL_BLOCK)))
    forced = (blk == 0) | (blk == cur) | (blk == cur - 1)
    score = jnp.where(blk * SEL_BLOCK <= tpos, imp + FORCE_BONUS * forced.astype(F32), -1e9)
    sel = _top_k_mask(score, blk.astype(F32), 0, min(SEL_TOP_N, nsel))
    hide_scr[...] = jnp.where(sel > 0.5, 0.0, NEG)

    per_sub = tq // SEL_BLOCK
    init = (jnp.full((1, lanes), NEG, F32), jnp.zeros((dh + V_AUG, lanes), F32))

    def rel_tile(rel):
        return jnp.where(rel < 0, 4, jnp.where(rel == 0, 0, jnp.where(rel == 1, 1, 3)))

    def sel_tile(j, carry):
        ks, vs, adds = [], [], []
        for s_i in range(sub):
            js = j * sub + s_i
            ks.append(ks_ref[0, 0, 0, pl.ds(pl.multiple_of(js * tq, tq), tq), :])
            vs.append(vst_ref[0, 0, 0, js])
            hide = jnp.concatenate(
                [jnp.broadcast_to(hide_scr[pl.ds(js * per_sub + b, 1), :], (SEL_BLOCK, tq)) for b in range(per_sub)],
                axis=0)
            adds.append(bd_ref[0, rel_tile(qi - js)] + jnp.concatenate([hide] * hpg, axis=1))
        s = logits(jnp.concatenate(ks, axis=0)) + jnp.concatenate(adds, axis=0)
        return _softmax_tile_t(carry, s, jnp.concatenate(vs, axis=1))

    jd = qi // sub
    carry = sel_tile(jd, init)
    _, acc = lax.fori_loop(0, jd, sel_tile, carry)
    l = acc[dh:dh + 1]
    osel = acc[0:dh] / jnp.where(l > 0, l, 1.0)

    n_back = WINDOW // tq
    ks, vs, adds = [], [], []
    for dt in range(n_back, -1, -1):
        j = qi - dt
        jc = jnp.maximum(j, 0)
        tile = 0 if dt == 0 else 1 if dt == 1 else 2 if dt == n_back else 3
        ks.append(kw_ref[0, 0, 0, pl.ds(pl.multiple_of(jc * tq, tq), tq), :])
        vs.append(vwt_ref[0, 0, 0, jc])
        adds.append(bd_ref[0, tile] + jnp.where(j >= 0, 0.0, NEG))
    s = logits(jnp.concatenate(ks, axis=0)) + jnp.concatenate(adds, axis=0)
    _, acc = _softmax_tile_t(init, s, jnp.concatenate(vs, axis=1))
    l = acc[dh:dh + 1]
    owin = acc[0:dh] / jnp.where(l > 0, l, 1.0)

    gt = jax.nn.sigmoid(gt_ref[0])
    outs = []
    for h in range(hpg):
        blk_h = slice(h * tq, (h + 1) * tq)
        outs.append(gt[3 * h:3 * h + 1] * oc[:, blk_h] + gt[3 * h + 1:3 * h + 2] * osel[:, blk_h]
                    + gt[3 * h + 2:3 * h + 3] * owin[:, blk_h])
    o_ref[...] = jnp.concatenate(outs, axis=0).T.astype(BF16)


def nsa_prompt_t(p, gates_t, kc, vct, kt, vt, cmp_bias, diag_bias, ovt, n, t, *, tq, sub):
    g, dh = NSA_KV_GROUPS, NSA_HEAD_DIM
    nq = t // tq
    hpg = cmp_bias.shape[3] // tq
    nc = kc.shape[2]
    nsel = ovt.shape[0]
    assert t % (tq * sub) == 0 and WINDOW % tq == 0 and tq % SEL_BLOCK == 0

    def k_spec(z):
        return pl.BlockSpec((1, 1, 1, t, dh), lambda b, gi, qi, z=z: (z, b, gi, 0, 0))

    def vt_spec(z):
        return pl.BlockSpec((1, 1, 1, nq, dh + V_AUG, tq), lambda b, gi, qi, z=z: (z, b, gi, 0, 0, 0))

    return pl.pallas_call(
        functools.partial(_nsa_prompt_t_kernel, tq=tq, hpg=hpg, sub=sub),
        grid=(n, g, nq),
        in_specs=[
            pl.BlockSpec((tq, hpg * dh), lambda b, gi, qi: (b * nq + qi, gi)),
            pl.BlockSpec((1, 3 * hpg, tq), lambda b, gi, qi: (gi, 0, b * nq + qi)),
            pl.BlockSpec((1, 1, nc, dh), lambda b, gi, qi: (b, gi, 0, 0)),
            pl.BlockSpec((1, 1, dh, nc), lambda b, gi, qi: (b, gi, 0, 0)),
            k_spec(0), vt_spec(0), k_spec(1), vt_spec(1),
            pl.BlockSpec((1, 1, nc, hpg * tq), lambda b, gi, qi: (gi, qi, 0, 0)),
            pl.BlockSpec((1, 5, tq, hpg * tq), lambda b, gi, qi: (gi, 0, 0, 0)),
            pl.BlockSpec(ovt.shape, lambda b, gi, qi: (0, 0)),
        ],
        out_specs=pl.BlockSpec((tq, hpg * dh), lambda b, gi, qi: (b * nq + qi, gi)),
        out_shape=jax.ShapeDtypeStruct((n * t, g * hpg * dh), BF16),
        scratch_shapes=[pltpu.VMEM((nsel, tq), F32)],
        compiler_params=_params(("parallel", "parallel", "arbitrary")),
        name="nsa_prompt",
    )(p, gates_t, kc, vct, kt, vt, kt, vt, cmp_bias, diag_bias, ovt)


def _nsa_prompt_t_tables(rel_bias, t, tq, hpg):
    g = NSA_KV_GROUPS
    nc = t // CMP_STRIDE
    ns = -(-t // SEL_BLOCK)
    nq = t // tq

    def lanes(b):
        k = b.shape[1]
        return b.reshape(g, hpg, k, tq).transpose(0, 2, 1, 3).reshape(g, k, hpg * tq)

    tpos = jnp.arange(t)
    cmp_end = jnp.arange(nc) * CMP_STRIDE + CMP_BLOCK - 1
    dc = tpos[None, :] - cmp_end[:, None]
    cmp_bias = _dist_bias(rel_bias, dc, (dc >= 0) & (jnp.arange(nc) < nc - 1)[:, None])
    cmp_bias = cmp_bias.reshape(g, hpg, nc, nq, tq).transpose(0, 3, 2, 1, 4).reshape(g, nq, nc, hpg * tq)
    key = jnp.arange(tq)[:, None]
    qry = jnp.arange(tq)[None, :]
    d0 = qry - key
    d1 = tq + qry - key
    dw = WINDOW + qry - key
    far = jnp.full((tq, tq), 2 * REL_MAX_DIST)
    assert tq >= REL_MAX_DIST
    tiles = [
        lanes(_dist_bias(rel_bias, d0, d0 >= 0)),
        lanes(_dist_bias(rel_bias, d1, d1 >= 0)),
        lanes(_dist_bias(rel_bias, dw, dw < WINDOW)),
        lanes(_dist_bias(rel_bias, far, far >= 0)),
        lanes(_dist_bias(rel_bias, far, far < 0)),
    ]
    diag_bias = jnp.stack(tiles, axis=1)
    ovt = jnp.concatenate([_cmp_sel_overlap(nc - 1, ns), jnp.zeros((1, ns), F32)], axis=0).T
    return cmp_bias, diag_bias, ovt


def _attend_t(tiles, qbd, s_scr):
    m = None
    for idx, (k, _, add) in enumerate(tiles):
        s = jnp.dot(k.astype(BF16), qbd, preferred_element_type=F32) + add
        s_scr[idx] = s
        mt = jnp.max(s, axis=0, keepdims=True)
        m = mt if m is None else jnp.maximum(m, mt)
    l = acc = None
    for idx, (_, v, _) in enumerate(tiles):
        et = jnp.exp(s_scr[idx] - m).T
        lt = jnp.sum(et, axis=1, keepdims=True)
        pv = jnp.dot(et.astype(BF16), v.astype(BF16), preferred_element_type=F32)
        l = lt if l is None else l + lt
        acc = pv if acc is None else acc + pv
    return acc / jnp.where(l > 0, l, 1.0)


def _nsa_sample_kernel(pt_ref, *refs, n_pages, page, tnew, ns):
    del pt_ref
    pages = refs[:n_pages]
    (ksn_ref, vsn_ref, kwn_ref, vwn_ref, win_ref, qbd_ref, gt_ref, cbt_ref, sbt_ref, wbt_ref,
     ovt_ref, hs_ref, w1_ref, w2_ref, b_ref, o_ref, s_scr) = refs[n_pages:]
    g, dh = NSA_KV_GROUPS, NSA_HEAD_DIM
    gd = g * dh
    rows = qbd_ref.shape[2]
    rpg = rows // g
    qbd = qbd_ref[0]

    cpt = gd // LANES
    pcb = NSA_CACHED * cpt
    wcb = 2 * cpt

    def wide(ref, cbs, tok0, n_tok, tok_stride, cb0):
        return jnp.concatenate(
            [ref[0, 0, pl.ds(tok0 * cbs + cb, n_tok, stride=tok_stride * cbs), :] for cb in range(cb0, cb0 + cpt)],
            axis=1)

    def cached(z):
        return lambda i: jnp.concatenate(
            [wide(pg, pcb, i, page // CMP_STRIDE, CMP_STRIDE, z * cpt) for pg in pages], axis=0)

    kc = _compress(cached(0), w1_ref, w2_ref, b_ref, 0)
    vc = _compress(cached(1), w1_ref, w2_ref, b_ref, 1)

    st = jnp.dot(kc.astype(BF16), qbd, preferred_element_type=F32) + cbt_ref[...]
    m = jnp.max(st, axis=0, keepdims=True)
    e = jnp.where(cbt_ref[...] > VIS, jnp.exp(st - m), 0.0)
    den = jnp.sum(e, axis=0, keepdims=True)
    pct = e / jnp.where(den > 0, den, 1.0)
    oc = jnp.dot(pct.T.astype(BF16), vc.astype(BF16), preferred_element_type=F32)

    imp = _dot_f32(_dot_f32(ovt_ref[...], pct), hs_ref[...])
    nsp = imp.shape[0]
    blk = lax.broadcasted_iota(jnp.int32, (nsp, rows), 0)
    cur = ns - 1
    forced = (blk == 0) | (blk == cur) | (blk == cur - 1)
    score = jnp.where(blk < ns, imp + FORCE_BONUS * forced.astype(F32), LOWEST)
    sel = _top_k_mask(score, blk.astype(F32), 0, min(SEL_TOP_N, ns))
    hide = jnp.where(sel > 0.5, 0.0, NEG)

    def block_rows(b):
        return jnp.broadcast_to(hide[b:b + 1], (SEL_BLOCK, rows))

    pad = jnp.zeros((page - tnew, gd), F32)
    per_page = page // SEL_BLOCK
    tiles = []
    for k, pg in enumerate(pages):
        add = sbt_ref[k] + jnp.concatenate([block_rows(per_page * k + b) for b in range(per_page)], axis=0)
        tiles.append((wide(pg, pcb, 0, page, 1, 2 * cpt), wide(pg, pcb, 0, page, 1, 3 * cpt), add))
    add = sbt_ref[n_pages] + jnp.concatenate(
        [block_rows(min(per_page * n_pages + b, nsp - 1)) for b in range(per_page)], axis=0)
    tiles.append((jnp.concatenate([ksn_ref[...], pad], axis=0),
                  jnp.concatenate([vsn_ref[...], pad], axis=0), add))
    osel = _attend_t(tiles, qbd, s_scr)

    wbuf = win_ref.shape[2] // wcb
    tiles = []
    for k in range(wbuf // page):
        tiles.append((wide(win_ref, wcb, k * page, page, 1, 0), wide(win_ref, wcb, k * page, page, 1, cpt),
                      wbt_ref[k]))
    tiles.append((jnp.concatenate([kwn_ref[...], pad], axis=0),
                  jnp.concatenate([vwn_ref[...], pad], axis=0), wbt_ref[wbuf // page]))
    owin = _attend_t(tiles, qbd, s_scr)

    gt = jax.nn.sigmoid(gt_ref[0])
    o = gt[:, 0:1] * oc + gt[:, 1:2] * osel + gt[:, 2:3] * owin
    for gi in range(g):
        o_ref[0, gi] = o[gi * rpg:(gi + 1) * rpg, gi * dh:(gi + 1) * dh].astype(BF16)


def nsa_sample(p, row0, cache_kv, layer, cache_win, page_table, qbd, gates, tables, cweights, *, tnew):
    n, n_pages = page_table.shape
    g, dh = NSA_KV_GROUPS, NSA_HEAD_DIM
    gd = g * dh
    page = cache_kv.shape[2]
    rows = qbd.shape[2]
    cbt, sbt, wbt, ovt, hs = tables
    w1, w2, hid0 = cweights
    ns = -(-(n_pages * page + tnew) // SEL_BLOCK)
    assert row0 % tnew == 0 and page % SEL_BLOCK == 0 and page == LANES and tnew <= page
    assert (n_pages * page + tnew) // CMP_STRIDE == n_pages * page // CMP_STRIDE
    blk0 = row0 // tnew
    q_blocks = (rows // tnew * dh) // gd
    assert gd % LANES == 0
    prow = page * NSA_CACHED * gd // LANES
    cache = cache_kv.reshape(cache_kv.shape[0], cache_kv.shape[1], prow, LANES)
    wbuf = cache_win.shape[2]
    wrow = wbuf * 2 * gd // LANES
    win = cache_win.reshape(cache_win.shape[0], n, wrow, LANES)
    assert wbuf % page == 0

    def page_spec(k):
        return pl.BlockSpec((1, 1, prow, LANES), lambda i, pt, k=k: (layer, pt[i * n_pages + k], 0, 0))

    def new_spec(z):
        return pl.BlockSpec((tnew, gd), lambda i, pt, z=z: (blk0 + i, q_blocks + z))

    def const(shape):
        return pl.BlockSpec(shape, lambda i, pt: (0,) * len(shape))

    grid_spec = pltpu.PrefetchScalarGridSpec(
        num_scalar_prefetch=1,
        grid=(n,),
        in_specs=[page_spec(k) for k in range(n_pages)] + [
            new_spec(2), new_spec(3), new_spec(4), new_spec(5),
            pl.BlockSpec((1, 1, wrow, LANES), lambda i, pt: (layer, i, 0, 0)),
            pl.BlockSpec((1, gd, rows), lambda i, pt: (i, 0, 0)),
            pl.BlockSpec((1, rows, 3), lambda i, pt: (i, 0, 0)),
            const(cbt.shape), const(sbt.shape), const(wbt.shape), const(ovt.shape), const(hs.shape),
            const(w1.shape), const(w2.shape), const(hid0.shape),
        ],
        out_specs=pl.BlockSpec((1, g, rows // g, dh), lambda i, pt: (i, 0, 0, 0)),
        scratch_shapes=[pltpu.VMEM((n_pages + 1, page, rows), F32)],
    )
    return pl.pallas_call(
        functools.partial(_nsa_sample_kernel, n_pages=n_pages, page=page, tnew=tnew, ns=ns),
        grid_spec=grid_spec,
        out_shape=jax.ShapeDtypeStruct((n, g, rows // g, dh), BF16),
        compiler_params=_params(("arbitrary",)),
        name="nsa_sample",
    )(page_table.reshape(-1), *([cache] * n_pages), p, p, p, p, win, qbd, gates,
      cbt, sbt, wbt, ovt, hs, w1, w2, hid0)


def _nsa_sample_tables(rel_bias, past, tnew, wbuf, page, hpg):
    g = NSA_KV_GROUPS
    h = g * hpg
    qpos = past + jnp.arange(tnew)
    n_str = (past + tnew) // CMP_STRIDE
    ns = -(-(past + tnew) // SEL_BLOCK)
    nsp = -(-ns // 8) * 8

    def lanes(b):
        return b.transpose(1, 0, 2).reshape(b.shape[1], h * tnew)

    cmp_end = jnp.arange(n_str) * CMP_STRIDE + CMP_BLOCK - 1
    dc = qpos[None, :] - cmp_end[:, None]
    cbt = lanes(_dist_bias(rel_bias, dc, (dc >= 0) & (jnp.arange(n_str) < n_str - 1)[:, None]))
    n_tiles = past // page + 1
    kpos = jnp.arange(n_tiles * page)
    ds = qpos[None, :] - kpos[:, None]
    sbt = lanes(_dist_bias(rel_bias, ds, (ds >= 0) & (kpos < past + tnew)[:, None]))
    sbt = sbt.reshape(n_tiles, page, h * tnew)
    w_tiles = wbuf // page + 1
    r = jnp.arange(w_tiles * page)
    dw = qpos[None, :] - (past - wbuf + r)[:, None]
    wbt = lanes(_dist_bias(rel_bias, dw, (dw >= 0) & (dw < WINDOW) & (r < wbuf + tnew)[:, None]))
    wbt = wbt.reshape(w_tiles, page, h * tnew)
    ov = _cmp_sel_overlap(n_str - 1, ns)
    ovt = jnp.zeros((nsp, n_str), F32).at[:ns, :n_str - 1].set(ov.T)
    lane = jnp.arange(h * tnew)
    same = (lane[:, None] // (hpg * tnew) == lane[None, :] // (hpg * tnew)) & \
           (lane[:, None] % tnew == lane[None, :] % tnew)
    return cbt, sbt, wbt, ovt, same.astype(F32)


def _softplus(x):
    return jnp.maximum(x, 0.0) + jnp.log1p(jnp.exp(-jnp.abs(x)))


def _unit_lower_inverse(a):
    c = a[0].shape[0]
    eye = (lax.broadcasted_iota(jnp.int32, (c, c), 0) == lax.broadcasted_iota(jnp.int32, (c, c), 1)).astype(F32)
    inv = [eye - x for x in a]
    power = a
    n = 2
    while n < c:
        power = [_dot(x, x) for x in power]
        inv = [y + _dot(y, x) for x, y in zip(power, inv)]
        n *= 2
    resid = [(eye - y) - _dot_split(x, y) for x, y in zip(a, inv)]
    return [y + _dot(y, r) for y, r in zip(inv, resid)]


def _gdn_kernel(qkv_ref, z_ref, ab_ref, hist_ref, s0_ref, cw_ref, nar_ref, dtr_ref, nac_ref, dtc_ref,
                nw_ref, o_ref, sfin_ref, s_scr, carry_scr, *, c, tv, qk_heads, v_heads):
    hd = GDN_HEAD
    halo = carry_scr.shape[0]
    step = pl.program_id(1)

    @pl.when(step == 0)
    def _():
        s_scr[...] = s0_ref[0]
        carry_scr[...] = hist_ref[0]

    x = qkv_ref[...]
    ext = jnp.concatenate([carry_scr[...], x], axis=0)
    off = halo - (GDN_CONV - 1)
    conv = ext[off:off + tv] * cw_ref[0:1]
    for i in range(1, GDN_CONV):
        conv = conv + ext[off + i:off + i + tv] * cw_ref[i:i + 1]
    carry_scr[...] = ext[tv:tv + halo]
    act = conv * jax.nn.sigmoid(conv)
    if tv < c:
        act = jnp.concatenate([act, jnp.zeros((c - tv, act.shape[1]), F32)], axis=0)
        ab = jnp.concatenate([ab_ref[...], jnp.zeros((c - tv, ab_ref.shape[1]), F32)], axis=0)
    else:
        ab = ab_ref[...]

    row = lax.broadcasted_iota(jnp.int32, (c, c), 0)
    col = lax.broadcasted_iota(jnp.int32, (c, c), 1)
    incl = row >= col
    strict = row > col
    lower = incl.astype(F32)
    upper = (row <= col).astype(F32)

    live_c = lax.broadcasted_iota(jnp.int32, ab.shape, 0) < tv
    g_col = jnp.where(live_c, nar_ref[...] * _softplus(ab + dtr_ref[...]), 0.0)
    beta = jnp.where(live_c, jax.nn.sigmoid(ab), 0.0)
    abt = ab.T[0:v_heads]
    live_r = lax.broadcasted_iota(jnp.int32, abt.shape, 1) < tv
    g_row = jnp.where(live_r, nac_ref[...] * _softplus(abt + dtc_ref[...]), 0.0)
    gam_col = _dot_f32(lower, g_col)
    gam_row = _dot_f32(g_row, upper)

    rep = v_heads // qk_heads
    hvs = range(v_heads)
    qn, kn, kk, qk = [], [], [], []
    for hq in range(qk_heads):
        qh = act[:, hq * hd:(hq + 1) * hd]
        kh = act[:, (qk_heads + hq) * hd:(qk_heads + hq + 1) * hd]
        qn.append(qh * lax.rsqrt(jnp.sum(qh * qh, axis=-1, keepdims=True) + 1e-6) * (hd ** -0.5))
        kn.append(kh * lax.rsqrt(jnp.sum(kh * kh, axis=-1, keepdims=True) + 1e-6))
    for hq in range(qk_heads):
        kk.append(_dot_nt(kn[hq], kn[hq]))
        qk.append(_dot_nt(qn[hq], kn[hq]))
    gc = [gam_col[:, hv:hv + 1] for hv in hvs]
    bc = [beta[:, v_heads + hv:v_heads + hv + 1] for hv in hvs]
    decay = [jnp.where(incl, jnp.exp(jnp.where(incl, gc[hv] - gam_row[hv:hv + 1, :], 0.0)), 0.0) for hv in hvs]
    a_mat = [jnp.where(strict, bc[hv] * kk[hv // rep] * decay[hv], 0.0) for hv in hvs]
    tinv = _unit_lower_inverse(a_mat)
    eg = [jnp.exp(gc[hv]) for hv in hvs]
    g_last = [gc[hv][c - 1:c, :] for hv in hvs]
    rhs = [jnp.concatenate([act[:, (2 * qk_heads + hv) * hd:(2 * qk_heads + hv + 1) * hd] * bc[hv],
                            kn[hv // rep] * (bc[hv] * eg[hv])], axis=1) for hv in hvs]
    vk = [_dot(tinv[hv], rhs[hv]) for hv in hvs]
    s_old = [s_scr[hv] for hv in hvs]
    ks = [_dot(jnp.concatenate([vk[hv][:, hd:], qn[hv // rep] * eg[hv]], axis=0), s_old[hv]) for hv in hvs]
    u = [vk[hv][:, :hd] - ks[hv][:c] for hv in hvs]
    k_dec = [(kn[hv // rep] * jnp.exp(g_last[hv] - gc[hv])).T for hv in hvs]
    upd = [_dot(jnp.concatenate([qk[hv // rep] * decay[hv], k_dec[hv]], axis=0), u[hv]) for hv in hvs]
    for hv in hvs:
        s_scr[hv] = s_old[hv] * jnp.exp(g_last[hv]) + upd[hv][c:]
        o = (ks[hv][c:] + upd[hv][:c])[0:tv]
        zh = z_ref[:, hv * hd:(hv + 1) * hd]
        o = _rms(o, nw_ref[...]) * (zh * jax.nn.sigmoid(zh))
        o_ref[:, hv * hd:(hv + 1) * hd] = o.astype(BF16)

    @pl.when(step == pl.num_programs(1) - 1)
    def _():
        sfin_ref[0] = s_scr[...]


def gdn(p, row0, n, t, hist, s0, conv_w, a_log, dt_bias, norm_w, *, qk_heads, v_heads):
    hd = GDN_HEAD
    ch = (2 * qk_heads + v_heads) * hd
    vw = v_heads * hd
    c = GDN_CHUNK
    tv = min(c, t)
    assert t % tv == 0 and row0 % tv == 0 and ch % vw == 0
    nchunk = t // tv
    blk0 = row0 // tv
    halo = 8
    histp = jnp.concatenate([jnp.zeros((n, halo - (GDN_CONV - 1), ch), F32), hist.astype(F32)], axis=1)
    cw = jnp.concatenate([conv_w.astype(F32), jnp.zeros((halo - GDN_CONV, ch), F32)], axis=0)
    neg_a = -jnp.exp(a_log.astype(F32))
    dt = dt_bias.astype(F32)
    pad = jnp.zeros((LANES - v_heads,), F32)
    nar = jnp.concatenate([neg_a, pad]).reshape(1, LANES)
    dtr = jnp.concatenate([dt, pad]).reshape(1, LANES)
    o, sfin = pl.pallas_call(
        functools.partial(_gdn_kernel, c=c, tv=tv, qk_heads=qk_heads, v_heads=v_heads),
        grid=(n, nchunk),
        in_specs=[
            pl.BlockSpec((tv, ch), lambda b, s: (blk0 + b * nchunk + s, 0)),
            pl.BlockSpec((tv, vw), lambda b, s: (blk0 + b * nchunk + s, ch // vw)),
            pl.BlockSpec((tv, LANES), lambda b, s: (blk0 + b * nchunk + s, (ch + vw) // LANES)),
            pl.BlockSpec((1, halo, ch), lambda b, s: (b, 0, 0)),
            pl.BlockSpec((1, v_heads, hd, hd), lambda b, s: (b, 0, 0, 0)),
            pl.BlockSpec((halo, ch), lambda b, s: (0, 0)),
            pl.BlockSpec((1, LANES), lambda b, s: (0, 0)),
            pl.BlockSpec((1, LANES), lambda b, s: (0, 0)),
            pl.BlockSpec((v_heads, 1), lambda b, s: (0, 0)),
            pl.BlockSpec((v_heads, 1), lambda b, s: (0, 0)),
            pl.BlockSpec((1, hd), lambda b, s: (0, 0)),
        ],
        out_specs=[
            pl.BlockSpec((tv, vw), lambda b, s: (b * nchunk + s, 0)),
            pl.BlockSpec((1, v_heads, hd, hd), lambda b, s: (b, 0, 0, 0)),
        ],
        out_shape=[jax.ShapeDtypeStruct((n * t, vw), BF16),
                   jax.ShapeDtypeStruct((n, v_heads, hd, hd), F32)],
        scratch_shapes=[pltpu.VMEM((v_heads, hd, hd), F32), pltpu.VMEM((halo, ch), F32)],
        compiler_params=_params(("parallel", "arbitrary")),
        name="gdn",
    )(p, p, p, histp, s0.astype(F32), cw, nar, dtr, neg_a.reshape(v_heads, 1), dt.reshape(v_heads, 1),
      norm_w.astype(F32).reshape(1, hd))
    return o, sfin


def _pow2_divisor(m, cap):
    tile = cap
    while m % tile:
        tile //= 2
    return tile


def _pad_cols(w, mult):
    n = w.shape[-1]
    return jnp.pad(w, ((0, 0), (0, -n % mult)))


def kernel(x_prompt, x_sample, cache_kv, cache_win, state_ssm, state_conv, page_table, rel_bias, norm_mix, norm_ffn, norm_final, nsa_w_in, nsa_w_out, nsa_cmp_w1, nsa_cmp_w2, nsa_cmp_pe, gdn_w_in, gdn_conv_w, gdn_a_log, gdn_dt_bias, gdn_norm_w, gdn_w_out, ffn_w1, ffn_w2):
    nb, t, d = x_prompt.shape
    ns_, tnew, _ = x_sample.shape
    depth = norm_mix.shape[0]
    g, dh = NSA_KV_GROUPS, NSA_HEAD_DIM
    gd = g * dh
    heads = nsa_w_out.shape[1] // dh
    hpg = heads // g
    qw = heads * dh
    mp = nb * t
    past = page_table.shape[1] * cache_kv.shape[2]
    wbuf = cache_win.shape[2]
    page = cache_kv.shape[2]
    v_heads = gdn_a_log.shape[1]
    qk_heads = (gdn_conv_w.shape[2] // GDN_HEAD - v_heads) // 2
    conv_ch = gdn_conv_w.shape[2]
    tn = 7 * LANES
    m_all = nb * t + ns_ * tnew
    tm = _pow2_divisor(m_all, 1024)
    tm_ffn = _pow2_divisor(m_all, 512)

    x = jnp.concatenate([x_prompt.reshape(mp, d), x_sample.reshape(ns_ * tnew, d)], axis=0).astype(F32)
    tq = LANES
    p_tables = _nsa_prompt_t_tables(rel_bias, t, tq, hpg)
    s_tables = _nsa_sample_tables(rel_bias, past, tnew, wbuf, page, hpg)

    kv_p, kv_s, win_p, win_s, ssm_p, ssm_s, conv_p, conv_s = [], [], [], [], [], [], [], []
    for i in range(depth):
        li = i // 2
        if i % 2 == 0:
            w_in = _pad_cols(nsa_w_in[li], tn).astype(BF16)
            p = norm_matmul(x, norm_mix[i], w_in, tm=tm, tn=tn)
            pp = p[:mp].reshape(nb, t, -1)
            ps = p[mp:].reshape(ns_, tnew, -1)
            kv_end = qw + NSA_N_KV * gd
            cweights = _compress_weights(nsa_cmp_w1[li], nsa_cmp_w2[li], nsa_cmp_pe[li])
            kc, vct = nsa_compress(p, nb, t, qw, *cweights)
            kv4 = pp[:, :, qw + 2 * gd:kv_end].reshape(nb, t, 2, 2, g, dh)
            kt = kv4[:, :, :, 0].transpose(2, 0, 3, 1, 4).astype(BF16)
            vt = kv4[:, :, :, 1].transpose(2, 0, 3, 4, 1).astype(BF16)
            aug = jnp.zeros((2, nb, g, V_AUG, t), BF16).at[:, :, :, 0].set(1.0)
            vt = jnp.concatenate([vt, aug], axis=3).reshape(2, nb, g, dh + V_AUG, t // tq, tq)
            vt = vt.transpose(0, 1, 2, 4, 3, 5)
            gates_p = pp[:, :, kv_end:kv_end + 3 * heads].reshape(mp, g, 3 * hpg).transpose(1, 2, 0)
            o_p = nsa_prompt_t(p, gates_p, kc, vct, kt, vt, *p_tables, nb, t, tq=tq, sub=4)
            q_s = (ps[:, :, :qw] * dh ** -0.5).reshape(ns_, tnew, g, hpg, dh)
            qbd = jnp.einsum("ntghd,gk->ngdkht", q_s, jnp.eye(g, dtype=F32))
            qbd = qbd.reshape(ns_, gd, heads * tnew).astype(BF16)
            gates_s = ps[:, :, kv_end:kv_end + 3 * heads].reshape(ns_, tnew, heads, 3)
            gates_s = gates_s.transpose(0, 2, 1, 3).reshape(ns_, heads * tnew, 3)
            o_s = nsa_sample(p, mp, cache_kv, li, cache_win, page_table, qbd, gates_s, s_tables, cweights,
                             tnew=tnew)
            o_s = o_s.reshape(ns_, g, hpg, tnew, dh).transpose(0, 3, 1, 2, 4).reshape(ns_ * tnew, qw)
            o = jnp.concatenate([o_p, o_s], axis=0)
            w_out = nsa_w_out[li].astype(BF16)
            kv_p.append(pp[:, :, qw:qw + NSA_CACHED * gd].reshape(nb, t, NSA_CACHED, g, dh))
            kv_s.append(ps[:, :, qw:qw + NSA_CACHED * gd].reshape(ns_, tnew, NSA_CACHED, g, dh))
            wkeep = min(WINDOW, t)
            win_p.append(pp[:, t - wkeep:, qw + NSA_CACHED * gd:kv_end].reshape(nb, wkeep, 2, g, dh))
            new_win = ps[:, :, qw + NSA_CACHED * gd:kv_end].reshape(ns_, tnew, 2, g, dh)
            win_s.append(jnp.concatenate([cache_win[li], new_win.astype(cache_win.dtype)], axis=1)[:, -wbuf:])
        else:
            w_in = _pad_cols(gdn_w_in[li], tn).astype(BF16)
            p = norm_matmul(x, norm_mix[i], w_in, tm=tm, tn=tn)
            pp = p[:mp].reshape(nb, t, -1)
            ps = p[mp:].reshape(ns_, tnew, -1)
            args = (gdn_conv_w[li], gdn_a_log[li], gdn_dt_bias[li], gdn_norm_w[li])
            h0 = jnp.zeros((nb, GDN_CONV - 1, conv_ch), F32)
            s0 = jnp.zeros((nb, v_heads, GDN_HEAD, GDN_HEAD), F32)
            o_p, sp = gdn(p, 0, nb, t, h0, s0, *args, qk_heads=qk_heads, v_heads=v_heads)
            o_s, ss = gdn(p, mp, ns_, tnew, state_conv[li], state_ssm[li], *args,
                          qk_heads=qk_heads, v_heads=v_heads)
            o = jnp.concatenate([o_p, o_s], axis=0)
            w_out = gdn_w_out[li].astype(BF16)
            ssm_p.append(sp.astype(state_ssm.dtype))
            ssm_s.append(ss.astype(state_ssm.dtype))
            keep = GDN_CONV - 1
            assert t >= keep and tnew >= keep
            conv_p.append(pp[:, t - keep:, :conv_ch])
            conv_s.append(ps[:, tnew - keep:, :conv_ch])
        x = mix_ffn(x, o, w_out, norm_ffn[i], ffn_w1[i].astype(BF16), ffn_w2[i].astype(BF16), norm_final,
                    final_norm=(i == depth - 1), tm=tm_ffn, th=512)
    y_prompt = x[:mp].reshape(nb, t, d)
    y_sample = x[mp:].reshape(ns_, tnew, d)
    return (y_prompt, y_sample, jnp.stack(kv_p), jnp.stack(kv_s), jnp.stack(win_p), jnp.stack(win_s),
            jnp.stack(ssm_p), jnp.stack(ssm_s), jnp.stack(conv_p), jnp.stack(conv_s))
```

```python
import functools
import math

import jax
import jax.numpy as jnp
from jax import lax
from jax.experimental import pallas as pl
from jax.experimental.pallas import tpu as pltpu

F32 = jnp.float32
BF16 = jnp.bfloat16
HIGHEST = lax.Precision.HIGHEST

LANES = 128
VMEM_LIMIT = 56 * 1024 * 1024

NSA_HEAD_DIM = 64
NSA_KV_GROUPS = 4
NSA_N_KV = 6
NSA_CACHED = 4
CMP_BLOCK = 32
CMP_STRIDE = 16
SEL_BLOCK = 64
SEL_TOP_N = 8
WINDOW = 512
FORCE_BONUS = 1e4
REL_BUCKETS = 32
REL_MAX_DIST = 128
GDN_HEAD = 128
GDN_CONV = 4
GDN_CHUNK = 64
RMS_EPS = 1e-6
NEG = -1e30
VIS = -1e29
LOWEST = -3e38

NT = (((1,), (1,)), ((), ()))


def _params(sem):
    return pltpu.CompilerParams(dimension_semantics=sem, vmem_limit_bytes=VMEM_LIMIT)


def _dot(a, b):
    return jnp.dot(a.astype(BF16), b.astype(BF16), preferred_element_type=F32)


def _dot_nt(a, b):
    return lax.dot_general(a.astype(BF16), b.astype(BF16), NT, preferred_element_type=F32)


def _dot_split(a, b):
    ah, bh = a.astype(BF16), b.astype(BF16)
    al = (a - ah.astype(F32)).astype(BF16)
    bl = (b - bh.astype(F32)).astype(BF16)
    hh = jnp.dot(ah, bh, preferred_element_type=F32)
    return hh + (jnp.dot(ah, bl, preferred_element_type=F32) + jnp.dot(al, bh, preferred_element_type=F32))


def _dot_f32(a, b):
    return jnp.dot(a, b, precision=HIGHEST, preferred_element_type=F32)


def _rms(x, gain):
    ms = jnp.mean(x * x, axis=-1, keepdims=True)
    return x * lax.rsqrt(ms + RMS_EPS) * gain


def _norm_matmul_kernel(x_ref, g_ref, w_ref, o_ref, xn_ref):
    @pl.when(pl.program_id(1) == 0)
    def _():
        xn_ref[...] = _rms(x_ref[...], g_ref[...]).astype(BF16)

    o_ref[...] = jnp.dot(xn_ref[...], w_ref[...], preferred_element_type=F32)


def norm_matmul(x, gain, w, *, tm, tn):
    m, d = x.shape
    n = w.shape[1]
    assert m % tm == 0 and n % tn == 0
    return pl.pallas_call(
        _norm_matmul_kernel,
        grid=(m // tm, n // tn),
        in_specs=[
            pl.BlockSpec((tm, d), lambda i, j: (i, 0)),
            pl.BlockSpec((1, d), lambda i, j: (0, 0)),
            pl.BlockSpec((d, tn), lambda i, j: (0, j)),
        ],
        out_specs=pl.BlockSpec((tm, tn), lambda i, j: (i, j)),
        out_shape=jax.ShapeDtypeStruct((m, n), F32),
        scratch_shapes=[pltpu.VMEM((tm, d), BF16)],
        compiler_params=_params(("parallel", "arbitrary")),
        name="norm_matmul",
    )(x, gain.reshape(1, d), w)


def _mix_ffn_kernel(x_ref, o_ref, wo_ref, g_ref, w1_ref, w2_ref, gf_ref, y_ref,
                    x1_ref, xn_ref, acc_ref, *, final_norm):
    j = pl.program_id(1)

    @pl.when(j == 0)
    def _():
        x1 = x_ref[...] + jnp.dot(o_ref[...], wo_ref[...], preferred_element_type=F32)
        x1_ref[...] = x1
        xn_ref[...] = _rms(x1, g_ref[...]).astype(BF16)
        acc_ref[...] = jnp.zeros_like(acc_ref)

    hid = jnp.maximum(jnp.dot(xn_ref[...], w1_ref[...], preferred_element_type=F32), 0.0)
    acc_ref[...] += jnp.dot((hid * hid).astype(BF16), w2_ref[...], preferred_element_type=F32)

    @pl.when(j == pl.num_programs(1) - 1)
    def _():
        y = x1_ref[...] + acc_ref[...]
        if final_norm:
            y = _rms(y, gf_ref[...])
        y_ref[...] = y


def mix_ffn(x, o, w_out, gain, w1, w2, gain_final, *, final_norm, tm, th):
    m, d = x.shape
    ko = o.shape[1]
    hdim = w1.shape[1]
    assert m % tm == 0 and hdim % th == 0
    return pl.pallas_call(
        functools.partial(_mix_ffn_kernel, final_norm=final_norm),
        grid=(m // tm, hdim // th),
        in_specs=[
            pl.BlockSpec((tm, d), lambda i, j: (i, 0)),
            pl.BlockSpec((tm, ko), lambda i, j: (i, 0)),
            pl.BlockSpec((ko, d), lambda i, j: (0, 0)),
            pl.BlockSpec((1, d), lambda i, j: (0, 0)),
            pl.BlockSpec((d, th), lambda i, j: (0, j)),
            pl.BlockSpec((th, d), lambda i, j: (j, 0)),
            pl.BlockSpec((1, d), lambda i, j: (0, 0)),
        ],
        out_specs=pl.BlockSpec((tm, d), lambda i, j: (i, 0)),
        out_shape=jax.ShapeDtypeStruct((m, d), F32),
        scratch_shapes=[pltpu.VMEM((tm, d), F32), pltpu.VMEM((tm, d), BF16), pltpu.VMEM((tm, d), F32)],
        compiler_params=_params(("parallel", "arbitrary")),
        name="mix_ffn",
    )(x, o, w_out, gain.reshape(1, d), w1, w2, gain_final.reshape(1, d))


def _rel_bucket(dist):
    n = jnp.maximum(dist, 0)
    max_exact = REL_BUCKETS // 2
    nf = jnp.maximum(n, 1).astype(F32)
    large = max_exact + (jnp.log(nf / max_exact) / math.log(REL_MAX_DIST / max_exact)
                         * (REL_BUCKETS - max_exact)).astype(jnp.int32)
    return jnp.where(n < max_exact, n, jnp.minimum(large, REL_BUCKETS - 1))


def _dist_bias(rel_bias, dist, visible):
    tbl = rel_bias.astype(F32)
    onehot = (_rel_bucket(dist)[..., None] == jnp.arange(REL_BUCKETS)).astype(F32)
    b = jnp.einsum("...k,kh->h...", onehot, tbl, precision=HIGHEST)
    return jnp.where(visible[None], b, NEG)


def _cmp_sel_overlap(nc, ns):
    c0 = jnp.arange(nc) * CMP_STRIDE
    s0 = jnp.arange(ns) * SEL_BLOCK
    ov = jnp.minimum(c0[:, None] + CMP_BLOCK, s0[None, :] + SEL_BLOCK) - jnp.maximum(c0[:, None], s0[None, :])
    return jnp.maximum(ov, 0).astype(F32) / CMP_BLOCK


def _compress_weights(cw1, cw2, cpe):
    g = NSA_KV_GROUPS
    eye = jnp.eye(g, dtype=F32)
    w1 = jnp.einsum("zide,gh->zigdhe", cw1.astype(F32), eye)
    w1 = w1.reshape(2, CMP_BLOCK, g * NSA_HEAD_DIM, g * NSA_HEAD_DIM).astype(BF16)
    w2 = jnp.einsum("zde,gh->zgdhe", cw2.astype(F32), eye)
    w2 = w2.reshape(2, g * NSA_HEAD_DIM, g * NSA_HEAD_DIM).astype(BF16)
    hid0 = jnp.einsum("zid,zide->ze", cpe.astype(F32), cw1.astype(F32), precision=HIGHEST)
    hid0 = jnp.tile(hid0[:, None, :], (1, 1, g))
    return w1, w2, hid0


def _compress(load_rows, w1_ref, w2_ref, b_ref, z):
    r = CMP_BLOCK // CMP_STRIDE
    assert r == 2
    acc_a = acc_b = None
    for i in range(CMP_STRIDE):
        xi = load_rows(i).astype(BF16)
        da = jnp.dot(xi, w1_ref[z, i], preferred_element_type=F32)
        db = jnp.dot(xi, w1_ref[z, CMP_STRIDE + i], preferred_element_type=F32)
        acc_a = da if acc_a is None else acc_a + da
        acc_b = db if acc_b is None else acc_b + db
    n_str = acc_a.shape[0]
    hid = acc_a + pltpu.roll(acc_b, n_str - 1, axis=0) + b_ref[z]
    act = hid * jax.nn.sigmoid(hid)
    return jnp.dot(act.astype(BF16), w2_ref[z], preferred_element_type=F32)


def _top_k_mask(score, index, axis, k):
    sel = jnp.zeros(score.shape, F32)
    for _ in range(k):
        mx = jnp.max(score, axis=axis, keepdims=True)
        cand = jnp.where(score == mx, index, 1e9)
        first = jnp.min(cand, axis=axis, keepdims=True)
        hit = index == first
        sel = jnp.where(hit, 1.0, sel)
        score = jnp.where(hit, LOWEST, score)
    return sel


def _nsa_compress_kernel(*refs, n_str, n_half):
    g, dh = NSA_KV_GROUPS, NSA_HEAD_DIM
    srcs = (refs[:n_half], refs[n_half:2 * n_half])
    w1_ref, w2_ref, b_ref, kc_ref, vc_ref = refs[2 * n_half:]
    for z, dst in enumerate((kc_ref, vc_ref)):
        def load_rows(i, z=z):
            return jnp.concatenate([r[pl.ds(i, n_str, stride=CMP_STRIDE), :] for r in srcs[z]], axis=1)

        res = _compress(load_rows, w1_ref, w2_ref, b_ref, z)
        if z == 1:
            res = res.T
        for gi in range(g):
            dst[0, gi] = (res[gi * dh:(gi + 1) * dh] if z == 1 else res[:, gi * dh:(gi + 1) * dh]).astype(BF16)


def nsa_compress(p, n, t, qw, w1, w2, hid0):
    gd = NSA_KV_GROUPS * NSA_HEAD_DIM
    n_str = t // CMP_STRIDE
    assert qw % LANES == 0 and gd % LANES == 0
    n_half = gd // LANES
    q_blocks = qw // LANES
    out = [jax.ShapeDtypeStruct((n, NSA_KV_GROUPS, n_str, NSA_HEAD_DIM), BF16),
           jax.ShapeDtypeStruct((n, NSA_KV_GROUPS, NSA_HEAD_DIM, n_str), BF16)]
    return pl.pallas_call(
        functools.partial(_nsa_compress_kernel, n_str=n_str, n_half=n_half),
        grid=(n,),
        in_specs=[pl.BlockSpec((t, LANES), lambda i, c=c: (i, q_blocks + c)) for c in range(2 * n_half)] + [
            pl.BlockSpec(w1.shape, lambda i: (0, 0, 0, 0)),
            pl.BlockSpec(w2.shape, lambda i: (0, 0, 0)),
            pl.BlockSpec(hid0.shape, lambda i: (0, 0, 0)),
        ],
        out_specs=[pl.BlockSpec((1, NSA_KV_GROUPS, n_str, NSA_HEAD_DIM), lambda i: (i, 0, 0, 0)),
                   pl.BlockSpec((1, NSA_KV_GROUPS, NSA_HEAD_DIM, n_str), lambda i: (i, 0, 0, 0))],
        out_shape=out,
        compiler_params=_params(("parallel",)),
        name="nsa_compress",
    )(*([p] * (2 * n_half)), w1, w2, hid0)


V_AUG = 16


def _softmax_tile_t(carry, s, vt):
    m, acc = carry
    m_new = jnp.maximum(m, jnp.max(s, axis=0, keepdims=True))
    p = jnp.exp(s - m_new).astype(BF16)
    acc = jnp.exp(m - m_new) * acc + jnp.dot(vt, p, preferred_element_type=F32)
    return m_new, acc


def _nsa_prompt_t_kernel(q_ref, gt_ref, kc_ref, vct_ref, ks_ref, vst_ref, kw_ref, vwt_ref,
                         cb_ref, bd_ref, ovt_ref, o_ref, hide_scr, *, tq, hpg, sub):
    dh = NSA_HEAD_DIM
    qi = pl.program_id(2)
    nsel = ovt_ref.shape[0]
    lanes = hpg * tq
    q = q_ref[...] * (dh ** -0.5)
    q4 = jnp.concatenate([q[:, h * dh:(h + 1) * dh] for h in range(hpg)], axis=0).astype(BF16)

    def logits(k):
        return lax.dot_general(k, q4, NT, preferred_element_type=F32)

    cb = cb_ref[0, 0]
    s = logits(kc_ref[0, 0]) + cb
    m = jnp.max(s, axis=0, keepdims=True)
    e = jnp.where(cb > VIS, jnp.exp(s - m), 0.0)
    den = jnp.sum(e, axis=0, keepdims=True)
    pc = e / jnp.where(den > 0, den, 1.0)
    oc = jnp.dot(vct_ref[0, 0], pc.astype(BF16), preferred_element_type=F32)

    pcs = pc[:, 0:tq]
    for h in range(1, hpg):
        pcs = pcs + pc[:, h * tq:(h + 1) * tq]
    imp = _dot_f32(ovt_ref[...], pcs)
    blk = lax.broadcasted_iota(jnp.int32, (nsel, tq), 0)
    tpos = qi * tq + lax.broadcasted_iota(jnp.int32, (nsel, tq), 1)
    cur = lax.shift_right_logical(tpos, int(math.log2(SEL_BLOCK)))
    forced = (blk == 0) | (blk == cur) | (blk == cur - 1)
    score = jnp.where(blk * SEL_BLOCK <= tpos, imp + FORCE_BONUS * forced.astype(F32), -1e9)
    sel = _top_k_mask(score, blk.astype(F32), 0, min(SEL_TOP_N, nsel))
    hide_scr[...] = jnp.where(sel > 0.5, 0.0, NEG)

    per_sub = tq // SEL_BLOCK
    init = (jnp.full((1, lanes), NEG, F32), jnp.zeros((dh + V_AUG, lanes), F32))

    def rel_tile(rel):
        return jnp.where(rel < 0, 4, jnp.where(rel == 0, 0, jnp.where(rel == 1, 1, 3)))

    def sel_tile(j, carry):
        ks, vs, adds = [], [], []
        for s_i in range(sub):
            js = j * sub + s_i
            ks.append(ks_ref[0, 0, 0, pl.ds(pl.multiple_of(js * tq, tq), tq), :])
            vs.append(vst_ref[0, 0, 0, js])
            hide = jnp.concatenate(
                [jnp.broadcast_to(hide_scr[pl.ds(js * per_sub + b, 1), :], (SEL_BLOCK, tq)) for b in range(per_sub)],
                axis=0)
            adds.append(bd_ref[0, rel_tile(qi - js)] + jnp.concatenate([hide] * hpg, axis=1))
        s = logits(jnp.concatenate(ks, axis=0)) + jnp.concatenate(adds, axis=0)
        return _softmax_tile_t(carry, s, jnp.concatenate(vs, axis=1))

    jd = qi // sub
    carry = sel_tile(jd, init)
    _, acc = lax.fori_loop(0, jd, sel_tile, carry)
    l = acc[dh:dh + 1]
    osel = acc[0:dh] / jnp.where(l > 0, l, 1.0)

    n_back = WINDOW // tq
    ks, vs, adds = [], [], []
    for dt in range(n_back, -1, -1):
        j = qi - dt
        jc = jnp.maximum(j, 0)
        tile = 0 if dt == 0 else 1 if dt == 1 else 2 if dt == n_back else 3
        ks.append(kw_ref[0, 0, 0, pl.ds(pl.multiple_of(jc * tq, tq), tq), :])
        vs.append(vwt_ref[0, 0, 0, jc])
        adds.append(bd_ref[0, tile] + jnp.where(j >= 0, 0.0, NEG))
    s = logits(jnp.concatenate(ks, axis=0)) + jnp.concatenate(adds, axis=0)
    _, acc = _softmax_tile_t(init, s, jnp.concatenate(vs, axis=1))
    l = acc[dh:dh + 1]
    owin = acc[0:dh] / jnp.where(l > 0, l, 1.0)

    gt = jax.nn.sigmoid(gt_ref[0])
    outs = []
    for h in range(hpg):
        blk_h = slice(h * tq, (h + 1) * tq)
        outs.append(gt[3 * h:3 * h + 1] * oc[:, blk_h] + gt[3 * h + 1:3 * h + 2] * osel[:, blk_h]
                    + gt[3 * h + 2:3 * h + 3] * owin[:, blk_h])
    o_ref[...] = jnp.concatenate(outs, axis=0).T.astype(BF16)


def nsa_prompt_t(p, gates_t, kc, vct, kt, vt, cmp_bias, diag_bias, ovt, n, t, *, tq, sub):
    g, dh = NSA_KV_GROUPS, NSA_HEAD_DIM
    nq = t // tq
    hpg = cmp_bias.shape[3] // tq
    nc = kc.shape[2]
    nsel = ovt.shape[0]
    assert t % (tq * sub) == 0 and WINDOW % tq == 0 and tq % SEL_BLOCK == 0

    def k_spec(z):
        return pl.BlockSpec((1, 1, 1, t, dh), lambda b, gi, qi, z=z: (z, b, gi, 0, 0))

    def vt_spec(z):
        return pl.BlockSpec((1, 1, 1, nq, dh + V_AUG, tq), lambda b, gi, qi, z=z: (z, b, gi, 0, 0, 0))

    return pl.pallas_call(
        functools.partial(_nsa_prompt_t_kernel, tq=tq, hpg=hpg, sub=sub),
        grid=(n, g, nq),
        in_specs=[
            pl.BlockSpec((tq, hpg * dh), lambda b, gi, qi: (b * nq + qi, gi)),
            pl.BlockSpec((1, 3 * hpg, tq), lambda b, gi, qi: (gi, 0, b * nq + qi)),
            pl.BlockSpec((1, 1, nc, dh), lambda b, gi, qi: (b, gi, 0, 0)),
            pl.BlockSpec((1, 1, dh, nc), lambda b, gi, qi: (b, gi, 0, 0)),
            k_spec(0), vt_spec(0), k_spec(1), vt_spec(1),
            pl.BlockSpec((1, 1, nc, hpg * tq), lambda b, gi, qi: (gi, qi, 0, 0)),
            pl.BlockSpec((1, 5, tq, hpg * tq), lambda b, gi, qi: (gi, 0, 0, 0)),
            pl.BlockSpec(ovt.shape, lambda b, gi, qi: (0, 0)),
        ],
        out_specs=pl.BlockSpec((tq, hpg * dh), lambda b, gi, qi: (b * nq + qi, gi)),
        out_shape=jax.ShapeDtypeStruct((n * t, g * hpg * dh), BF16),
        scratch_shapes=[pltpu.VMEM((nsel, tq), F32)],
        compiler_params=_params(("parallel", "parallel", "arbitrary")),
        name="nsa_prompt",
    )(p, gates_t, kc, vct, kt, vt, kt, vt, cmp_bias, diag_bias, ovt)


def _nsa_prompt_t_tables(rel_bias, t, tq, hpg):
    g = NSA_KV_GROUPS
    nc = t // CMP_STRIDE
    ns = -(-t // SEL_BLOCK)
    nq = t // tq

    def lanes(b):
        k = b.shape[1]
        return b.reshape(g, hpg, k, tq).transpose(0, 2, 1, 3).reshape(g, k, hpg * tq)

    tpos = jnp.arange(t)
    cmp_end = jnp.arange(nc) * CMP_STRIDE + CMP_BLOCK - 1
    dc = tpos[None, :] - cmp_end[:, None]
    cmp_bias = _dist_bias(rel_bias, dc, (dc >= 0) & (jnp.arange(nc) < nc - 1)[:, None])
    cmp_bias = cmp_bias.reshape(g, hpg, nc, nq, tq).transpose(0, 3, 2, 1, 4).reshape(g, nq, nc, hpg * tq)
    key = jnp.arange(tq)[:, None]
    qry = jnp.arange(tq)[None, :]
    d0 = qry - key
    d1 = tq + qry - key
    dw = WINDOW + qry - key
    far = jnp.full((tq, tq), 2 * REL_MAX_DIST)
    assert tq >= REL_MAX_DIST
    tiles = [
        lanes(_dist_bias(rel_bias, d0, d0 >= 0)),
        lanes(_dist_bias(rel_bias, d1, d1 >= 0)),
        lanes(_dist_bias(rel_bias, dw, dw < WINDOW)),
        lanes(_dist_bias(rel_bias, far, far >= 0)),
        lanes(_dist_bias(rel_bias, far, far < 0)),
    ]
    diag_bias = jnp.stack(tiles, axis=1)
    ovt = jnp.concatenate([_cmp_sel_overlap(nc - 1, ns), jnp.zeros((1, ns), F32)], axis=0).T
    return cmp_bias, diag_bias, ovt


def _attend_rows(tiles, q4, s_scr):
    m = None
    for idx, (k, _, add, transposed) in enumerate(tiles):
        kb = k.astype(BF16)
        s = (jnp.dot(q4, kb, preferred_element_type=F32) if transposed
             else lax.dot_general(q4, kb, NT, preferred_element_type=F32)) + add
        s_scr[idx] = s
        mt = jnp.max(s, axis=1, keepdims=True)
        m = mt if m is None else jnp.maximum(m, mt)
    l = acc = None
    for idx, (_, v, _, transposed) in enumerate(tiles):
        e = jnp.exp(s_scr[idx] - m)
        lt = jnp.sum(e, axis=1, keepdims=True)
        eb, vb = e.astype(BF16), v.astype(BF16)
        pv = (lax.dot_general(eb, vb, NT, preferred_element_type=F32) if transposed
              else jnp.dot(eb, vb, preferred_element_type=F32))
        l = lt if l is None else l + lt
        acc = pv if acc is None else acc + pv
    return acc / jnp.where(l > 0, l, 1.0)


def _nsa_sample_kernel(pt_ref, *refs, n_pages, page, tnew, ns):
    del pt_ref
    pages = refs[:n_pages]
    (ksn_ref, vsn_ref, kwn_ref, vwn_ref, win_ref, q4_ref, gt_ref, cb_ref, sb_ref, wb_ref,
     ov_ref, hs_ref, ex_ref, w1_ref, w2_ref, b_ref, o_ref, s_scr, xt_scr) = refs[n_pages:]
    g, dh = NSA_KV_GROUPS, NSA_HEAD_DIM
    gd = g * dh
    rows = q4_ref.shape[1]
    rpg = rows // g
    q4 = q4_ref[0]
    cpt = gd // LANES

    spp = page // CMP_STRIDE
    n_str = n_pages * spp
    r = lax.broadcasted_iota(jnp.int32, (page, page), 0)
    t_in = lax.broadcasted_iota(jnp.int32, (page, page), 1)
    assert spp & (spp - 1) == 0
    stride_of_r = jnp.bitwise_and(r, spp - 1)
    offset_of_r = lax.shift_right_logical(r, int(math.log2(spp)))
    perm = (t_in == stride_of_r * CMP_STRIDE + offset_of_r).astype(BF16)
    for z in range(2):
        for k, pg in enumerate(pages):
            for c in range(cpt):
                slab = pg[0, 0, z, pl.ds(c * LANES, LANES), :].astype(BF16)
                rows_by_offset = lax.dot_general(perm, slab, NT, preferred_element_type=F32)
                for i in range(CMP_STRIDE):
                    xt_scr[z, c, pl.ds(i * n_str + k * spp, spp), :] = rows_by_offset[i * spp:(i + 1) * spp]

    def cached(z):
        return lambda i: jnp.concatenate([xt_scr[z, c, pl.ds(i * n_str, n_str), :] for c in range(cpt)], axis=1)

    kc = _compress(cached(0), w1_ref, w2_ref, b_ref, 0)
    vc = _compress(cached(1), w1_ref, w2_ref, b_ref, 1)

    cb = cb_ref[...]
    s = lax.dot_general(q4, kc.astype(BF16), NT, preferred_element_type=F32) + cb
    m = jnp.max(s, axis=1, keepdims=True)
    e = jnp.where(cb > VIS, jnp.exp(s - m), 0.0)
    den = jnp.sum(e, axis=1, keepdims=True)
    pc = e / jnp.where(den > 0, den, 1.0)
    oc = jnp.dot(pc.astype(BF16), vc.astype(BF16), preferred_element_type=F32)

    imp = _dot_f32(hs_ref[...], _dot_f32(pc, ov_ref[...]))
    nsp = imp.shape[1]
    blk = lax.broadcasted_iota(jnp.int32, (rows, nsp), 1)
    cur = ns - 1
    forced = (blk == 0) | (blk == cur) | (blk == cur - 1)
    score = jnp.where(blk < ns, imp + FORCE_BONUS * forced.astype(F32), LOWEST)
    nsb = -(-ns // 8) * 8
    sel_t = _top_k_mask(score.T[0:nsb], blk.T[0:nsb].astype(F32), 0, min(SEL_TOP_N, ns))
    sel = jnp.concatenate([sel_t, jnp.zeros((nsp - nsb, rows), F32)], axis=0).T.astype(BF16)

    def hide(k):
        return jnp.where(jnp.dot(sel, ex_ref[k], preferred_element_type=F32) > 0.5, 0.0, NEG)

    pad = jnp.zeros((page - tnew, gd), F32)
    tiles = [(pg[0, 0, 2], pg[0, 0, 3], sb_ref[k] + hide(k), True) for k, pg in enumerate(pages)]
    tiles.append((jnp.concatenate([ksn_ref[...], pad], axis=0), jnp.concatenate([vsn_ref[...], pad], axis=0),
                  sb_ref[n_pages] + hide(n_pages), False))
    osel = _attend_rows(tiles, q4, s_scr)

    w_tiles = win_ref.shape[4] // page
    tiles = [(win_ref[0, 0, 0, :, pl.ds(k * page, page)], win_ref[0, 0, 1, :, pl.ds(k * page, page)], wb_ref[k], True)
             for k in range(w_tiles)]
    tiles.append((jnp.concatenate([kwn_ref[...], pad], axis=0), jnp.concatenate([vwn_ref[...], pad], axis=0),
                  wb_ref[w_tiles], False))
    owin = _attend_rows(tiles, q4, s_scr)

    gt = jax.nn.sigmoid(gt_ref[0])
    o = gt[:, 0:1] * oc + gt[:, 1:2] * osel + gt[:, 2:3] * owin
    for gi in range(g):
        o_ref[0, gi] = o[gi * rpg:(gi + 1) * rpg, gi * dh:(gi + 1) * dh].astype(BF16)


def nsa_sample(p, row0, cache_t, layer, win_t, page_table, q4, gates, tables, cweights, *, tnew):
    n, n_pages = page_table.shape
    g, dh = NSA_KV_GROUPS, NSA_HEAD_DIM
    gd = g * dh
    page = cache_t.shape[4]
    rows = q4.shape[1]
    cb, sb, wb, ov, hs, ex = tables
    w1, w2, hid0 = cweights
    ns = -(-(n_pages * page + tnew) // SEL_BLOCK)
    assert row0 % tnew == 0 and page % SEL_BLOCK == 0 and page == LANES and tnew <= page
    assert (n_pages * page + tnew) // CMP_STRIDE == n_pages * page // CMP_STRIDE
    blk0 = row0 // tnew
    q_blocks = (rows // tnew * dh) // gd
    assert gd % LANES == 0
    wbuf = win_t.shape[4]
    assert wbuf % page == 0

    def page_spec(k):
        return pl.BlockSpec((1, 1, NSA_CACHED, gd, page), lambda i, pt, k=k: (layer, pt[i * n_pages + k], 0, 0, 0))

    def new_spec(z):
        return pl.BlockSpec((tnew, gd), lambda i, pt, z=z: (blk0 + i, q_blocks + z))

    def const(shape):
        return pl.BlockSpec(shape, lambda i, pt: (0,) * len(shape))

    grid_spec = pltpu.PrefetchScalarGridSpec(
        num_scalar_prefetch=1,
        grid=(n,),
        in_specs=[page_spec(k) for k in range(n_pages)] + [
            new_spec(2), new_spec(3), new_spec(4), new_spec(5),
            pl.BlockSpec((1, 1, 2, gd, wbuf), lambda i, pt: (layer, i, 0, 0, 0)),
            pl.BlockSpec((1, rows, gd), lambda i, pt: (i, 0, 0)),
            pl.BlockSpec((1, rows, 3), lambda i, pt: (i, 0, 0)),
            const(cb.shape), const(sb.shape), const(wb.shape), const(ov.shape), const(hs.shape), const(ex.shape),
            const(w1.shape), const(w2.shape), const(hid0.shape),
        ],
        out_specs=pl.BlockSpec((1, g, rows // g, dh), lambda i, pt: (i, 0, 0, 0)),
        scratch_shapes=[pltpu.VMEM((n_pages + 1, rows, page), F32),
                        pltpu.VMEM((2, gd // LANES, n_pages * page, LANES), F32)],
    )
    return pl.pallas_call(
        functools.partial(_nsa_sample_kernel, n_pages=n_pages, page=page, tnew=tnew, ns=ns),
        grid_spec=grid_spec,
        out_shape=jax.ShapeDtypeStruct((n, g, rows // g, dh), BF16),
        compiler_params=_params(("arbitrary",)),
        name="nsa_sample",
    )(page_table.reshape(-1), *([cache_t] * n_pages), p, p, p, p, win_t, q4, gates,
      cb, sb, wb, ov, hs, ex, w1, w2, hid0)


def _nsa_sample_tables(rel_bias, past, tnew, wbuf, page, hpg):
    g = NSA_KV_GROUPS
    h = g * hpg
    qpos = past + jnp.arange(tnew)
    n_str = (past + tnew) // CMP_STRIDE
    ns = -(-(past + tnew) // SEL_BLOCK)
    nsp = LANES
    assert ns <= nsp

    def rows(b):
        return b.reshape(h * tnew, b.shape[2])

    def tiled(b, n_tiles):
        return b.reshape(h * tnew, n_tiles, page).transpose(1, 0, 2)

    cmp_end = jnp.arange(n_str) * CMP_STRIDE + CMP_BLOCK - 1
    dc = qpos[:, None] - cmp_end[None, :]
    cb = rows(_dist_bias(rel_bias, dc, (dc >= 0) & (jnp.arange(n_str) < n_str - 1)[None, :]))
    n_tiles = past // page + 1
    kpos = jnp.arange(n_tiles * page)
    ds = qpos[:, None] - kpos[None, :]
    sb = tiled(rows(_dist_bias(rel_bias, ds, (ds >= 0) & (kpos < past + tnew)[None, :])), n_tiles)
    w_tiles = wbuf // page + 1
    r = jnp.arange(w_tiles * page)
    dw = qpos[:, None] - (past - wbuf + r)[None, :]
    wb = tiled(rows(_dist_bias(rel_bias, dw, (dw >= 0) & (dw < WINDOW) & (r < wbuf + tnew)[None, :])), w_tiles)
    ov = jnp.zeros((n_str, nsp), F32).at[:n_str - 1, :ns].set(_cmp_sel_overlap(n_str - 1, ns))
    row = jnp.arange(h * tnew)
    same = (row[:, None] // (hpg * tnew) == row[None, :] // (hpg * tnew)) & \
           (row[:, None] % tnew == row[None, :] % tnew)
    expand = (kpos[None, :] // SEL_BLOCK == jnp.arange(nsp)[:, None]).astype(BF16)
    expand = expand.reshape(nsp, n_tiles, page).transpose(1, 0, 2)
    return cb, sb, wb, ov, same.astype(F32), expand


def _softplus(x):
    return jnp.maximum(x, 0.0) + jnp.log1p(jnp.exp(-jnp.abs(x)))


def _unit_lower_inverse(a):
    c = a[0].shape[0]
    eye = (lax.broadcasted_iota(jnp.int32, (c, c), 0) == lax.broadcasted_iota(jnp.int32, (c, c), 1)).astype(F32)
    inv = [eye - x for x in a]
    power = a
    n = 2
    while n < c:
        power = [_dot(x, x) for x in power]
        inv = [y + _dot(y, x) for x, y in zip(power, inv)]
        n *= 2
    resid = [(eye - y) - _dot_split(x, y) for x, y in zip(a, inv)]
    return [y + _dot(y, r) for y, r in zip(inv, resid)]


def _gdn_kernel(qkv_ref, z_ref, ab_ref, hist_ref, s0_ref, cw_ref, nar_ref, dtr_ref, nac_ref, dtc_ref,
                nw_ref, o_ref, sfin_ref, s_scr, carry_scr, *, c, tv, qk_heads, v_heads):
    hd = GDN_HEAD
    halo = carry_scr.shape[0]
    step = pl.program_id(1)

    @pl.when(step == 0)
    def _():
        s_scr[...] = s0_ref[0, 0]
        carry_scr[...] = hist_ref[0]

    x = qkv_ref[...]
    ext = jnp.concatenate([carry_scr[...], x], axis=0)
    off = halo - (GDN_CONV - 1)
    conv = ext[off:off + tv] * cw_ref[0:1]
    for i in range(1, GDN_CONV):
        conv = conv + ext[off + i:off + i + tv] * cw_ref[i:i + 1]
    carry_scr[...] = ext[tv:tv + halo]
    act = conv * jax.nn.sigmoid(conv)
    if tv < c:
        act = jnp.concatenate([act, jnp.zeros((c - tv, act.shape[1]), F32)], axis=0)
        ab = jnp.concatenate([ab_ref[...], jnp.zeros((c - tv, ab_ref.shape[1]), F32)], axis=0)
    else:
        ab = ab_ref[...]

    row = lax.broadcasted_iota(jnp.int32, (c, c), 0)
    col = lax.broadcasted_iota(jnp.int32, (c, c), 1)
    incl = row >= col
    strict = row > col
    lower = incl.astype(F32)
    upper = (row <= col).astype(F32)

    live_c = lax.broadcasted_iota(jnp.int32, ab.shape, 0) < tv
    g_col = jnp.where(live_c, nar_ref[...] * _softplus(ab + dtr_ref[...]), 0.0)
    beta = jnp.where(live_c, jax.nn.sigmoid(ab), 0.0)
    abt = ab.T[0:v_heads]
    live_r = lax.broadcasted_iota(jnp.int32, abt.shape, 1) < tv
    g_row = jnp.where(live_r, nac_ref[...] * _softplus(abt + dtc_ref[...]), 0.0)
    gam_col = _dot_f32(lower, g_col)
    gam_row = _dot_f32(g_row, upper)

    rep = v_heads // qk_heads
    hvs = range(v_heads)
    qn, kn, kk, qk = [], [], [], []
    for hq in range(qk_heads):
        qh = act[:, hq * hd:(hq + 1) * hd]
        kh = act[:, (qk_heads + hq) * hd:(qk_heads + hq + 1) * hd]
        qn.append(qh * lax.rsqrt(jnp.sum(qh * qh, axis=-1, keepdims=True) + 1e-6) * (hd ** -0.5))
        kn.append(kh * lax.rsqrt(jnp.sum(kh * kh, axis=-1, keepdims=True) + 1e-6))
    for hq in range(qk_heads):
        kk.append(_dot_nt(kn[hq], kn[hq]))
        qk.append(_dot_nt(qn[hq], kn[hq]))
    gc = [gam_col[:, hv:hv + 1] for hv in hvs]
    bc = [beta[:, v_heads + hv:v_heads + hv + 1] for hv in hvs]
    decay = [jnp.where(incl, jnp.exp(jnp.where(incl, gc[hv] - gam_row[hv:hv + 1, :], 0.0)), 0.0) for hv in hvs]
    a_mat = [jnp.where(strict, bc[hv] * kk[hv // rep] * decay[hv], 0.0) for hv in hvs]
    tinv = _unit_lower_inverse(a_mat)
    eg = [jnp.exp(gc[hv]) for hv in hvs]
    g_last = [gc[hv][c - 1:c, :] for hv in hvs]
    rhs = [jnp.concatenate([act[:, (2 * qk_heads + hv) * hd:(2 * qk_heads + hv + 1) * hd] * bc[hv],
                            kn[hv // rep] * (bc[hv] * eg[hv])], axis=1) for hv in hvs]
    vk = [_dot(tinv[hv], rhs[hv]) for hv in hvs]
    s_old = [s_scr[hv] for hv in hvs]
    ks = [_dot(jnp.concatenate([vk[hv][:, hd:], qn[hv // rep] * eg[hv]], axis=0), s_old[hv]) for hv in hvs]
    u = [vk[hv][:, :hd] - ks[hv][:c] for hv in hvs]
    k_dec = [(kn[hv // rep] * jnp.exp(g_last[hv] - gc[hv])).T for hv in hvs]
    upd = [_dot(jnp.concatenate([qk[hv // rep] * decay[hv], k_dec[hv]], axis=0), u[hv]) for hv in hvs]
    for hv in hvs:
        s_scr[hv] = s_old[hv] * jnp.exp(g_last[hv]) + upd[hv][c:]
        o = (ks[hv][c:] + upd[hv][:c])[0:tv]
        zh = z_ref[:, hv * hd:(hv + 1) * hd]
        o = _rms(o, nw_ref[...]) * (zh * jax.nn.sigmoid(zh))
        o_ref[:, hv * hd:(hv + 1) * hd] = o.astype(BF16)

    @pl.when(step == pl.num_programs(1) - 1)
    def _():
        sfin_ref[0] = s_scr[...]


def gdn(p, row0, n, t, hist, s0, layer, conv_w, a_log, dt_bias, norm_w, *, qk_heads, v_heads):
    hd = GDN_HEAD
    ch = (2 * qk_heads + v_heads) * hd
    vw = v_heads * hd
    c = GDN_CHUNK
    tv = min(c, t)
    assert t % tv == 0 and row0 % tv == 0 and ch % vw == 0
    nchunk = t // tv
    blk0 = row0 // tv
    halo = 8
    histp = jnp.concatenate([jnp.zeros((n, halo - (GDN_CONV - 1), ch), F32), hist.astype(F32)], axis=1)
    cw = jnp.concatenate([conv_w.astype(F32), jnp.zeros((halo - GDN_CONV, ch), F32)], axis=0)
    neg_a = -jnp.exp(a_log.astype(F32))
    dt = dt_bias.astype(F32)
    pad = jnp.zeros((LANES - v_heads,), F32)
    nar = jnp.concatenate([neg_a, pad]).reshape(1, LANES)
    dtr = jnp.concatenate([dt, pad]).reshape(1, LANES)
    o, sfin = pl.pallas_call(
        functools.partial(_gdn_kernel, c=c, tv=tv, qk_heads=qk_heads, v_heads=v_heads),
        grid=(n, nchunk),
        in_specs=[
            pl.BlockSpec((tv, ch), lambda b, s: (blk0 + b * nchunk + s, 0)),
            pl.BlockSpec((tv, vw), lambda b, s: (blk0 + b * nchunk + s, ch // vw)),
            pl.BlockSpec((tv, LANES), lambda b, s: (blk0 + b * nchunk + s, (ch + vw) // LANES)),
            pl.BlockSpec((1, halo, ch), lambda b, s: (b, 0, 0)),
            pl.BlockSpec((1, 1, v_heads, hd, hd), lambda b, s: (layer, b, 0, 0, 0)),
            pl.BlockSpec((halo, ch), lambda b, s: (0, 0)),
            pl.BlockSpec((1, LANES), lambda b, s: (0, 0)),
            pl.BlockSpec((1, LANES), lambda b, s: (0, 0)),
            pl.BlockSpec((v_heads, 1), lambda b, s: (0, 0)),
            pl.BlockSpec((v_heads, 1), lambda b, s: (0, 0)),
            pl.BlockSpec((1, hd), lambda b, s: (0, 0)),
        ],
        out_specs=[
            pl.BlockSpec((tv, vw), lambda b, s: (b * nchunk + s, 0)),
            pl.BlockSpec((1, v_heads, hd, hd), lambda b, s: (b, 0, 0, 0)),
        ],
        out_shape=[jax.ShapeDtypeStruct((n * t, vw), BF16),
                   jax.ShapeDtypeStruct((n, v_heads, hd, hd), F32)],
        scratch_shapes=[pltpu.VMEM((v_heads, hd, hd), F32), pltpu.VMEM((halo, ch), F32)],
        compiler_params=_params(("parallel", "arbitrary")),
        name="gdn",
    )(p, p, p, histp, s0.astype(F32), cw, nar, dtr, neg_a.reshape(v_heads, 1), dt.reshape(v_heads, 1),
      norm_w.astype(F32).reshape(1, hd))
    return o, sfin


def _pow2_divisor(m, cap):
    tile = cap
    while m % tile:
        tile //= 2
    return tile


def _window(p, r0, r1, c0, c1):
    return lax.slice(p, (r0, c0), (r1, c1))


def _pad_cols(w, mult):
    n = w.shape[-1]
    return jnp.pad(w, ((0, 0), (0, -n % mult)))


def kernel(x_prompt, x_sample, cache_kv, cache_win, state_ssm, state_conv, page_table, rel_bias, norm_mix, norm_ffn, norm_final, nsa_w_in, nsa_w_out, nsa_cmp_w1, nsa_cmp_w2, nsa_cmp_pe, gdn_w_in, gdn_conv_w, gdn_a_log, gdn_dt_bias, gdn_norm_w, gdn_w_out, ffn_w1, ffn_w2):
    nb, t, d = x_prompt.shape
    ns_, tnew, _ = x_sample.shape
    depth = norm_mix.shape[0]
    g, dh = NSA_KV_GROUPS, NSA_HEAD_DIM
    gd = g * dh
    heads = nsa_w_out.shape[1] // dh
    hpg = heads // g
    qw = heads * dh
    mp = nb * t
    past = page_table.shape[1] * cache_kv.shape[2]
    wbuf = cache_win.shape[2]
    page = cache_kv.shape[2]
    v_heads = gdn_a_log.shape[1]
    qk_heads = (gdn_conv_w.shape[2] // GDN_HEAD - v_heads) // 2
    conv_ch = gdn_conv_w.shape[2]
    tn = 7 * LANES
    m_all = nb * t + ns_ * tnew
    tm = _pow2_divisor(m_all, 1024)
    tm_ffn = _pow2_divisor(m_all, 512)

    x = jnp.concatenate([x_prompt.reshape(mp, d), x_sample.reshape(ns_ * tnew, d)], axis=0).astype(F32)
    tq = LANES
    p_tables = _nsa_prompt_t_tables(rel_bias, t, tq, hpg)
    s_tables = _nsa_sample_tables(rel_bias, past, tnew, wbuf, page, hpg)
    cache_t = cache_kv.transpose(0, 1, 3, 4, 5, 2).reshape(cache_kv.shape[0], cache_kv.shape[1], NSA_CACHED, gd, page)
    win_t = cache_win.transpose(0, 1, 3, 4, 5, 2).reshape(cache_win.shape[0], ns_, 2, gd, wbuf)

    kv_p, kv_s, win_p, win_s, ssm_p, ssm_s, conv_p, conv_s = [], [], [], [], [], [], [], []
    for i in range(depth):
        li = i // 2
        if i % 2 == 0:
            w_in = _pad_cols(nsa_w_in[li], tn).astype(BF16)
            p = norm_matmul(x, norm_mix[i], w_in, tm=tm, tn=tn)
            kv_end = qw + NSA_N_KV * gd
            ps = _window(p, mp, m_all, 0, kv_end + 3 * heads).reshape(ns_, tnew, -1)
            cweights = _compress_weights(nsa_cmp_w1[li], nsa_cmp_w2[li], nsa_cmp_pe[li])
            kc, vct = nsa_compress(p, nb, t, qw, *cweights)
            kv4 = _window(p, 0, mp, qw + 2 * gd, kv_end).reshape(nb, t, 2, 2, g, dh)
            kt = kv4[:, :, :, 0].transpose(2, 0, 3, 1, 4).astype(BF16)
            vt = kv4[:, :, :, 1].transpose(2, 0, 3, 4, 1).astype(BF16)
            aug = jnp.zeros((2, nb, g, V_AUG, t), BF16).at[:, :, :, 0].set(1.0)
            vt = jnp.concatenate([vt, aug], axis=3).reshape(2, nb, g, dh + V_AUG, t // tq, tq)
            vt = vt.transpose(0, 1, 2, 4, 3, 5)
            gates_p = _window(p, 0, mp, kv_end, kv_end + 3 * heads).reshape(mp, g, 3 * hpg).transpose(1, 2, 0)
            o_p = nsa_prompt_t(p, gates_p, kc, vct, kt, vt, *p_tables, nb, t, tq=tq, sub=4)
            q_s = (ps[:, :, :qw] * dh ** -0.5).reshape(ns_, tnew, g, hpg, dh)
            q4 = jnp.einsum("ntghd,gk->nghtkd", q_s, jnp.eye(g, dtype=F32))
            q4 = q4.reshape(ns_, heads * tnew, gd).astype(BF16)
            gates_s = ps[:, :, kv_end:kv_end + 3 * heads].reshape(ns_, tnew, heads, 3)
            gates_s = gates_s.transpose(0, 2, 1, 3).reshape(ns_, heads * tnew, 3)
            o_s = nsa_sample(p, mp, cache_t, li, win_t, page_table, q4, gates_s, s_tables, cweights, tnew=tnew)
            o_s = o_s.reshape(ns_, g, hpg, tnew, dh).transpose(0, 3, 1, 2, 4).reshape(ns_ * tnew, qw)
            o = jnp.concatenate([o_p, o_s], axis=0)
            w_out = nsa_w_out[li].astype(BF16)
            kv_p.append(_window(p, 0, mp, qw, qw + NSA_CACHED * gd).reshape(nb, t, NSA_CACHED, g, dh))
            kv_s.append(ps[:, :, qw:qw + NSA_CACHED * gd].reshape(ns_, tnew, NSA_CACHED, g, dh))
            wkeep = min(WINDOW, t)
            win_p.append(jnp.stack([_window(p, b * t + t - wkeep, (b + 1) * t, qw + NSA_CACHED * gd, kv_end)
                                    for b in range(nb)]).reshape(nb, wkeep, 2, g, dh))
            win_s.append(ps[:, :, qw + NSA_CACHED * gd:kv_end].reshape(ns_, tnew, 2, g, dh))
        else:
            w_in = _pad_cols(gdn_w_in[li], tn).astype(BF16)
            p = norm_matmul(x, norm_mix[i], w_in, tm=tm, tn=tn)
            args = (gdn_conv_w[li], gdn_a_log[li], gdn_dt_bias[li], gdn_norm_w[li])
            h0 = jnp.zeros((nb, GDN_CONV - 1, conv_ch), F32)
            s0 = jnp.zeros((1, nb, v_heads, GDN_HEAD, GDN_HEAD), F32)
            o_p, sp = gdn(p, 0, nb, t, h0, s0, 0, *args, qk_heads=qk_heads, v_heads=v_heads)
            o_s, ss = gdn(p, mp, ns_, tnew, state_conv[li], state_ssm, li, *args,
                          qk_heads=qk_heads, v_heads=v_heads)
            o = jnp.concatenate([o_p, o_s], axis=0)
            w_out = gdn_w_out[li].astype(BF16)
            ssm_p.append(sp.astype(state_ssm.dtype))
            ssm_s.append(ss.astype(state_ssm.dtype))
            keep = GDN_CONV - 1
            assert t >= keep and tnew >= keep
            conv_p.append(jnp.stack([_window(p, (b + 1) * t - keep, (b + 1) * t, 0, conv_ch) for b in range(nb)]))
            conv_s.append(_window(p, mp, m_all, 0, conv_ch).reshape(ns_, tnew, conv_ch)[:, tnew - keep:])
        x = mix_ffn(x, o, w_out, norm_ffn[i], ffn_w1[i].astype(BF16), ffn_w2[i].astype(BF16), norm_final,
                    final_norm=(i == depth - 1), tm=tm_ffn, th=512)
    y_prompt = x[:mp].reshape(nb, t, d)
    y_sample = x[mp:].reshape(ns_, tnew, d)
    new_win = jnp.stack(win_s).astype(cache_win.dtype)
    win_s_all = jnp.concatenate([cache_win[:new_win.shape[0]], new_win], axis=2)[:, :, -wbuf:]
    return (y_prompt, y_sample, jnp.stack(kv_p), jnp.stack(kv_s), jnp.stack(win_p), win_s_all,
            jnp.stack(ssm_p), jnp.stack(ssm_s), jnp.stack(conv_p), jnp.stack(conv_s))
```

```python
import functools
import math

import jax
import jax.numpy as jnp
from jax import lax
from jax.experimental import pallas as pl
from jax.experimental.pallas import tpu as pltpu

F32 = jnp.float32
BF16 = jnp.bfloat16
HIGHEST = lax.Precision.HIGHEST

LANES = 128
VMEM_LIMIT = 56 * 1024 * 1024

NSA_HEAD_DIM = 64
NSA_KV_GROUPS = 4
NSA_N_KV = 6
NSA_CACHED = 4
CMP_BLOCK = 32
CMP_STRIDE = 16
SEL_BLOCK = 64
SEL_TOP_N = 8
WINDOW = 512
FORCE_BONUS = 1e4
REL_BUCKETS = 32
REL_MAX_DIST = 128
GDN_HEAD = 128
GDN_CONV = 4
GDN_CHUNK = 64
RMS_EPS = 1e-6
NEG = -1e30
VIS = -1e29
LOWEST = -3e38

NT = (((1,), (1,)), ((), ()))


def _params(sem):
    return pltpu.CompilerParams(dimension_semantics=sem, vmem_limit_bytes=VMEM_LIMIT)


def _dot(a, b):
    return jnp.dot(a.astype(BF16), b.astype(BF16), preferred_element_type=F32)


def _dot_nt(a, b):
    return lax.dot_general(a.astype(BF16), b.astype(BF16), NT, preferred_element_type=F32)


def _dot_split(a, b):
    ah, bh = a.astype(BF16), b.astype(BF16)
    al = (a - ah.astype(F32)).astype(BF16)
    bl = (b - bh.astype(F32)).astype(BF16)
    hh = jnp.dot(ah, bh, preferred_element_type=F32)
    return hh + (jnp.dot(ah, bl, preferred_element_type=F32) + jnp.dot(al, bh, preferred_element_type=F32))


def _dot_f32(a, b):
    return jnp.dot(a, b, precision=HIGHEST, preferred_element_type=F32)


def _rms(x, gain):
    ms = jnp.mean(x * x, axis=-1, keepdims=True)
    return x * lax.rsqrt(ms + RMS_EPS) * gain


def _norm_matmul_kernel(x_ref, g_ref, w_ref, o_ref, xn_ref):
    @pl.when(pl.program_id(1) == 0)
    def _():
        xn_ref[...] = _rms(x_ref[...], g_ref[...]).astype(BF16)

    o_ref[...] = jnp.dot(xn_ref[...], w_ref[...], preferred_element_type=F32)


def norm_matmul(x, gain, w, *, tm, tn):
    m, d = x.shape
    n = w.shape[1]
    assert m % tm == 0 and n % tn == 0
    return pl.pallas_call(
        _norm_matmul_kernel,
        grid=(m // tm, n // tn),
        in_specs=[
            pl.BlockSpec((tm, d), lambda i, j: (i, 0)),
            pl.BlockSpec((1, d), lambda i, j: (0, 0)),
            pl.BlockSpec((d, tn), lambda i, j: (0, j)),
        ],
        out_specs=pl.BlockSpec((tm, tn), lambda i, j: (i, j)),
        out_shape=jax.ShapeDtypeStruct((m, n), F32),
        scratch_shapes=[pltpu.VMEM((tm, d), BF16)],
        compiler_params=_params(("parallel", "arbitrary")),
        name="norm_matmul",
    )(x, gain.reshape(1, d), w)


def _mix_ffn_kernel(x_ref, oa_ref, ob_ref, wo_ref, g_ref, w1_ref, w2_ref, gf_ref, y_ref,
                    x1_ref, xn_ref, acc_ref, *, final_norm, tiles_a):
    j = pl.program_id(1)

    @pl.when(j == 0)
    def _():
        o = jnp.where(pl.program_id(0) < tiles_a, oa_ref[...], ob_ref[...])
        x1 = x_ref[...] + jnp.dot(o, wo_ref[...], preferred_element_type=F32)
        x1_ref[...] = x1
        xn_ref[...] = _rms(x1, g_ref[...]).astype(BF16)
        acc_ref[...] = jnp.zeros_like(acc_ref)

    hid = jnp.maximum(jnp.dot(xn_ref[...], w1_ref[...], preferred_element_type=F32), 0.0)
    acc_ref[...] += jnp.dot((hid * hid).astype(BF16), w2_ref[...], preferred_element_type=F32)

    @pl.when(j == pl.num_programs(1) - 1)
    def _():
        y = x1_ref[...] + acc_ref[...]
        if final_norm:
            y = _rms(y, gf_ref[...])
        y_ref[...] = y


def mix_ffn(x, o_a, o_b, w_out, gain, w1, w2, gain_final, *, final_norm, tm, th):
    m, d = x.shape
    ko = o_a.shape[1]
    hdim = w1.shape[1]
    assert m % tm == 0 and hdim % th == 0 and o_a.shape[0] % tm == 0 and o_a.shape[0] + o_b.shape[0] == m
    tiles_a = o_a.shape[0] // tm
    return pl.pallas_call(
        functools.partial(_mix_ffn_kernel, final_norm=final_norm, tiles_a=tiles_a),
        grid=(m // tm, hdim // th),
        in_specs=[
            pl.BlockSpec((tm, d), lambda i, j: (i, 0)),
            pl.BlockSpec((tm, ko), lambda i, j: (jnp.minimum(i, tiles_a - 1), 0)),
            pl.BlockSpec((tm, ko), lambda i, j: (jnp.maximum(i - tiles_a, 0), 0)),
            pl.BlockSpec((ko, d), lambda i, j: (0, 0)),
            pl.BlockSpec((1, d), lambda i, j: (0, 0)),
            pl.BlockSpec((d, th), lambda i, j: (0, j)),
            pl.BlockSpec((th, d), lambda i, j: (j, 0)),
            pl.BlockSpec((1, d), lambda i, j: (0, 0)),
        ],
        out_specs=pl.BlockSpec((tm, d), lambda i, j: (i, 0)),
        out_shape=jax.ShapeDtypeStruct((m, d), F32),
        scratch_shapes=[pltpu.VMEM((tm, d), F32), pltpu.VMEM((tm, d), BF16), pltpu.VMEM((tm, d), F32)],
        compiler_params=_params(("parallel", "arbitrary")),
        name="mix_ffn",
    )(x, o_a, o_b, w_out, gain.reshape(1, d), w1, w2, gain_final.reshape(1, d))


def _rel_bucket(dist):
    n = jnp.maximum(dist, 0)
    max_exact = REL_BUCKETS // 2
    nf = jnp.maximum(n, 1).astype(F32)
    large = max_exact + (jnp.log(nf / max_exact) / math.log(REL_MAX_DIST / max_exact)
                         * (REL_BUCKETS - max_exact)).astype(jnp.int32)
    return jnp.where(n < max_exact, n, jnp.minimum(large, REL_BUCKETS - 1))


def _dist_bias(rel_bias, dist, visible):
    tbl = rel_bias.astype(F32)
    onehot = (_rel_bucket(dist)[..., None] == jnp.arange(REL_BUCKETS)).astype(F32)
    b = jnp.einsum("...k,kh->h...", onehot, tbl, precision=HIGHEST)
    return jnp.where(visible[None], b, NEG)


def _cmp_sel_overlap(nc, ns):
    c0 = jnp.arange(nc) * CMP_STRIDE
    s0 = jnp.arange(ns) * SEL_BLOCK
    ov = jnp.minimum(c0[:, None] + CMP_BLOCK, s0[None, :] + SEL_BLOCK) - jnp.maximum(c0[:, None], s0[None, :])
    return jnp.maximum(ov, 0).astype(F32) / CMP_BLOCK


def _compress_weights(cw1, cw2, cpe):
    g = NSA_KV_GROUPS
    eye = jnp.eye(g, dtype=F32)
    w1 = jnp.einsum("zide,gh->zigdhe", cw1.astype(F32), eye)
    w1 = w1.reshape(2, CMP_BLOCK, g * NSA_HEAD_DIM, g * NSA_HEAD_DIM).astype(BF16)
    w2 = jnp.einsum("zde,gh->zgdhe", cw2.astype(F32), eye)
    w2 = w2.reshape(2, g * NSA_HEAD_DIM, g * NSA_HEAD_DIM).astype(BF16)
    hid0 = jnp.einsum("zid,zide->ze", cpe.astype(F32), cw1.astype(F32), precision=HIGHEST)
    hid0 = jnp.tile(hid0[:, None, :], (1, 1, g))
    return w1, w2, hid0


def _compress(load_rows, w1_ref, w2_ref, b_ref, z):
    r = CMP_BLOCK // CMP_STRIDE
    assert r == 2
    acc_a = acc_b = None
    for i in range(CMP_STRIDE):
        xi = load_rows(i).astype(BF16)
        da = jnp.dot(xi, w1_ref[z, i], preferred_element_type=F32)
        db = jnp.dot(xi, w1_ref[z, CMP_STRIDE + i], preferred_element_type=F32)
        acc_a = da if acc_a is None else acc_a + da
        acc_b = db if acc_b is None else acc_b + db
    n_str = acc_a.shape[0]
    hid = acc_a + pltpu.roll(acc_b, n_str - 1, axis=0) + b_ref[z]
    act = hid * jax.nn.sigmoid(hid)
    return jnp.dot(act.astype(BF16), w2_ref[z], preferred_element_type=F32)


def _top_k_mask(score, index, axis, k):
    sel = jnp.zeros(score.shape, F32)
    for _ in range(k):
        mx = jnp.max(score, axis=axis, keepdims=True)
        cand = jnp.where(score == mx, index, 1e9)
        first = jnp.min(cand, axis=axis, keepdims=True)
        hit = index == first
        sel = jnp.where(hit, 1.0, sel)
        score = jnp.where(hit, LOWEST, score)
    return sel


def _nsa_compress_kernel(*refs, n_str, n_half):
    g, dh = NSA_KV_GROUPS, NSA_HEAD_DIM
    srcs = (refs[:n_half], refs[n_half:2 * n_half])
    w1_ref, w2_ref, b_ref, kc_ref, vc_ref = refs[2 * n_half:]
    for z, dst in enumerate((kc_ref, vc_ref)):
        def load_rows(i, z=z):
            return jnp.concatenate([r[pl.ds(i, n_str, stride=CMP_STRIDE), :] for r in srcs[z]], axis=1)

        res = _compress(load_rows, w1_ref, w2_ref, b_ref, z)
        if z == 1:
            res = res.T
        for gi in range(g):
            dst[0, gi] = (res[gi * dh:(gi + 1) * dh] if z == 1 else res[:, gi * dh:(gi + 1) * dh]).astype(BF16)


def nsa_compress(p, n, t, qw, w1, w2, hid0):
    gd = NSA_KV_GROUPS * NSA_HEAD_DIM
    n_str = t // CMP_STRIDE
    assert qw % LANES == 0 and gd % LANES == 0
    n_half = gd // LANES
    q_blocks = qw // LANES
    out = [jax.ShapeDtypeStruct((n, NSA_KV_GROUPS, n_str, NSA_HEAD_DIM), BF16),
           jax.ShapeDtypeStruct((n, NSA_KV_GROUPS, NSA_HEAD_DIM, n_str), BF16)]
    return pl.pallas_call(
        functools.partial(_nsa_compress_kernel, n_str=n_str, n_half=n_half),
        grid=(n,),
        in_specs=[pl.BlockSpec((t, LANES), lambda i, c=c: (i, q_blocks + c)) for c in range(2 * n_half)] + [
            pl.BlockSpec(w1.shape, lambda i: (0, 0, 0, 0)),
            pl.BlockSpec(w2.shape, lambda i: (0, 0, 0)),
            pl.BlockSpec(hid0.shape, lambda i: (0, 0, 0)),
        ],
        out_specs=[pl.BlockSpec((1, NSA_KV_GROUPS, n_str, NSA_HEAD_DIM), lambda i: (i, 0, 0, 0)),
                   pl.BlockSpec((1, NSA_KV_GROUPS, NSA_HEAD_DIM, n_str), lambda i: (i, 0, 0, 0))],
        out_shape=out,
        compiler_params=_params(("parallel",)),
        name="nsa_compress",
    )(*([p] * (2 * n_half)), w1, w2, hid0)


V_AUG = 16


def _softmax_tile_t(carry, s, vt):
    m, acc = carry
    m_new = jnp.maximum(m, jnp.max(s, axis=0, keepdims=True))
    p = jnp.exp(s - m_new).astype(BF16)
    acc = jnp.exp(m - m_new) * acc + jnp.dot(vt, p, preferred_element_type=F32)
    return m_new, acc


def _nsa_prompt_t_kernel(q_ref, gt_ref, kc_ref, vct_ref, ks_ref, vst_ref, kw_ref, vwt_ref,
                         cb_ref, bd_ref, ovt_ref, o_ref, hide_scr, *, tq, hpg, sub):
    dh = NSA_HEAD_DIM
    qi = pl.program_id(2)
    nsel = ovt_ref.shape[0]
    lanes = hpg * tq
    q = q_ref[...] * (dh ** -0.5)
    q4 = jnp.concatenate([q[:, h * dh:(h + 1) * dh] for h in range(hpg)], axis=0).astype(BF16)

    def logits(k):
        return lax.dot_general(k, q4, NT, preferred_element_type=F32)

    cb = cb_ref[0, 0]
    s = logits(kc_ref[0, 0]) + cb
    m = jnp.max(s, axis=0, keepdims=True)
    e = jnp.where(cb > VIS, jnp.exp(s - m), 0.0)
    den = jnp.sum(e, axis=0, keepdims=True)
    pc = e / jnp.where(den > 0, den, 1.0)
    oc = jnp.dot(vct_ref[0, 0], pc.astype(BF16), preferred_element_type=F32)

    pcs = pc[:, 0:tq]
    for h in range(1, hpg):
        pcs = pcs + pc[:, h * tq:(h + 1) * tq]
    imp = _dot_f32(ovt_ref[...], pcs)
    blk = lax.broadcasted_iota(jnp.int32, (nsel, tq), 0)
    tpos = qi * tq + lax.broadcasted_iota(jnp.int32, (nsel, tq), 1)
    cur = lax.shift_right_logical(tpos, int(math.log2(SEL_BLOCK)))
    forced = (blk == 0) | (blk == cur) | (blk == cur - 1)
    score = jnp.where(blk * SEL_BLOCK <= tpos, imp + FORCE_BONUS * forced.astype(F32), -1e9)
    sel = _top_k_mask(score, blk.astype(F32), 0, min(SEL_TOP_N, nsel))
    hide_scr[...] = jnp.where(sel > 0.5, 0.0, NEG)

    per_sub = tq // SEL_BLOCK
    init = (jnp.full((1, lanes), NEG, F32), jnp.zeros((dh + V_AUG, lanes), F32))

    def rel_tile(rel):
        return jnp.where(rel < 0, 4, jnp.where(rel == 0, 0, jnp.where(rel == 1, 1, 3)))

    def sel_tile(j, carry):
        ks, vs, adds = [], [], []
        for s_i in range(sub):
            js = j * sub + s_i
            ks.append(ks_ref[0, 0, 0, pl.ds(pl.multiple_of(js * tq, tq), tq), :])
            vs.append(vst_ref[0, 0, 0, js])
            hide = jnp.concatenate(
                [jnp.broadcast_to(hide_scr[pl.ds(js * per_sub + b, 1), :], (SEL_BLOCK, tq)) for b in range(per_sub)],
                axis=0)
            adds.append(bd_ref[0, rel_tile(qi - js)] + jnp.concatenate([hide] * hpg, axis=1))
        s = logits(jnp.concatenate(ks, axis=0)) + jnp.concatenate(adds, axis=0)
        return _softmax_tile_t(carry, s, jnp.concatenate(vs, axis=1))

    jd = qi // sub
    carry = sel_tile(jd, init)
    _, acc = lax.fori_loop(0, jd, sel_tile, carry)
    l = acc[dh:dh + 1]
    osel = acc[0:dh] / jnp.where(l > 0, l, 1.0)

    n_back = WINDOW // tq
    ks, vs, adds = [], [], []
    for dt in range(n_back, -1, -1):
        j = qi - dt
        jc = jnp.maximum(j, 0)
        tile = 0 if dt == 0 else 1 if dt == 1 else 2 if dt == n_back else 3
        ks.append(kw_ref[0, 0, 0, pl.ds(pl.multiple_of(jc * tq, tq), tq), :])
        vs.append(vwt_ref[0, 0, 0, jc])
        adds.append(bd_ref[0, tile] + jnp.where(j >= 0, 0.0, NEG))
    s = logits(jnp.concatenate(ks, axis=0)) + jnp.concatenate(adds, axis=0)
    _, acc = _softmax_tile_t(init, s, jnp.concatenate(vs, axis=1))
    l = acc[dh:dh + 1]
    owin = acc[0:dh] / jnp.where(l > 0, l, 1.0)

    gt = jax.nn.sigmoid(gt_ref[0])
    outs = []
    for h in range(hpg):
        blk_h = slice(h * tq, (h + 1) * tq)
        outs.append(gt[3 * h:3 * h + 1] * oc[:, blk_h] + gt[3 * h + 1:3 * h + 2] * osel[:, blk_h]
                    + gt[3 * h + 2:3 * h + 3] * owin[:, blk_h])
    o_ref[...] = jnp.concatenate(outs, axis=0).T.astype(BF16)


def nsa_prompt_t(p, gates_t, kc, vct, kt, vt, cmp_bias, diag_bias, ovt, n, t, *, tq, sub):
    g, dh = NSA_KV_GROUPS, NSA_HEAD_DIM
    nq = t // tq
    hpg = cmp_bias.shape[3] // tq
    nc = kc.shape[2]
    nsel = ovt.shape[0]
    assert t % (tq * sub) == 0 and WINDOW % tq == 0 and tq % SEL_BLOCK == 0

    def k_spec(z):
        return pl.BlockSpec((1, 1, 1, t, dh), lambda b, gi, qi, z=z: (z, b, gi, 0, 0))

    def vt_spec(z):
        return pl.BlockSpec((1, 1, 1, nq, dh + V_AUG, tq), lambda b, gi, qi, z=z: (z, b, gi, 0, 0, 0))

    return pl.pallas_call(
        functools.partial(_nsa_prompt_t_kernel, tq=tq, hpg=hpg, sub=sub),
        grid=(n, g, nq),
        in_specs=[
            pl.BlockSpec((tq, hpg * dh), lambda b, gi, qi: (b * nq + qi, gi)),
            pl.BlockSpec((1, 3 * hpg, tq), lambda b, gi, qi: (gi, 0, b * nq + qi)),
            pl.BlockSpec((1, 1, nc, dh), lambda b, gi, qi: (b, gi, 0, 0)),
            pl.BlockSpec((1, 1, dh, nc), lambda b, gi, qi: (b, gi, 0, 0)),
            k_spec(0), vt_spec(0), k_spec(1), vt_spec(1),
            pl.BlockSpec((1, 1, nc, hpg * tq), lambda b, gi, qi: (gi, qi, 0, 0)),
            pl.BlockSpec((1, 5, tq, hpg * tq), lambda b, gi, qi: (gi, 0, 0, 0)),
            pl.BlockSpec(ovt.shape, lambda b, gi, qi: (0, 0)),
        ],
        out_specs=pl.BlockSpec((tq, hpg * dh), lambda b, gi, qi: (b * nq + qi, gi)),
        out_shape=jax.ShapeDtypeStruct((n * t, g * hpg * dh), BF16),
        scratch_shapes=[pltpu.VMEM((nsel, tq), F32)],
        compiler_params=_params(("parallel", "parallel", "arbitrary")),
        name="nsa_prompt",
    )(p, gates_t, kc, vct, kt, vt, kt, vt, cmp_bias, diag_bias, ovt)


def _nsa_prompt_t_tables(rel_bias, t, tq, hpg):
    g = NSA_KV_GROUPS
    nc = t // CMP_STRIDE
    ns = -(-t // SEL_BLOCK)
    nq = t // tq

    def lanes(b):
        k = b.shape[1]
        return b.reshape(g, hpg, k, tq).transpose(0, 2, 1, 3).reshape(g, k, hpg * tq)

    tpos = jnp.arange(t)
    cmp_end = jnp.arange(nc) * CMP_STRIDE + CMP_BLOCK - 1
    dc = tpos[None, :] - cmp_end[:, None]
    cmp_bias = _dist_bias(rel_bias, dc, (dc >= 0) & (jnp.arange(nc) < nc - 1)[:, None])
    cmp_bias = cmp_bias.reshape(g, hpg, nc, nq, tq).transpose(0, 3, 2, 1, 4).reshape(g, nq, nc, hpg * tq)
    key = jnp.arange(tq)[:, None]
    qry = jnp.arange(tq)[None, :]
    d0 = qry - key
    d1 = tq + qry - key
    dw = WINDOW + qry - key
    far = jnp.full((tq, tq), 2 * REL_MAX_DIST)
    assert tq >= REL_MAX_DIST
    tiles = [
        lanes(_dist_bias(rel_bias, d0, d0 >= 0)),
        lanes(_dist_bias(rel_bias, d1, d1 >= 0)),
        lanes(_dist_bias(rel_bias, dw, dw < WINDOW)),
        lanes(_dist_bias(rel_bias, far, far >= 0)),
        lanes(_dist_bias(rel_bias, far, far < 0)),
    ]
    diag_bias = jnp.stack(tiles, axis=1)
    ovt = jnp.concatenate([_cmp_sel_overlap(nc - 1, ns), jnp.zeros((1, ns), F32)], axis=0).T
    return cmp_bias, diag_bias, ovt


def _attend_rows(tiles, q4, s_scr):
    m = None
    for idx, (k, _, add, transposed) in enumerate(tiles):
        kb = k.astype(BF16)
        s = (jnp.dot(q4, kb, preferred_element_type=F32) if transposed
             else lax.dot_general(q4, kb, NT, preferred_element_type=F32)) + add
        s_scr[idx] = s
        mt = jnp.max(s, axis=1, keepdims=True)
        m = mt if m is None else jnp.maximum(m, mt)
    l = acc = None
    for idx, (_, v, _, transposed) in enumerate(tiles):
        e = jnp.exp(s_scr[idx] - m)
        lt = jnp.sum(e, axis=1, keepdims=True)
        eb, vb = e.astype(BF16), v.astype(BF16)
        pv = (lax.dot_general(eb, vb, NT, preferred_element_type=F32) if transposed
              else jnp.dot(eb, vb, preferred_element_type=F32))
        l = lt if l is None else l + lt
        acc = pv if acc is None else acc + pv
    return acc / jnp.where(l > 0, l, 1.0)


def _nsa_sample_kernel(pt_ref, *refs, n_pages, page, tnew, ns):
    del pt_ref
    pages = refs[:n_pages]
    (ksn_ref, vsn_ref, kwn_ref, vwn_ref, win_ref, q4_ref, gt_ref, cb_ref, sb_ref, wb_ref,
     ov_ref, hs_ref, ex_ref, w1_ref, w2_ref, b_ref, o_ref, s_scr, xt_scr) = refs[n_pages:]
    g, dh = NSA_KV_GROUPS, NSA_HEAD_DIM
    gd = g * dh
    rows = q4_ref.shape[1]
    rpg = rows // g
    q4 = q4_ref[0]
    cpt = gd // LANES

    spp = page // CMP_STRIDE
    n_str = n_pages * spp
    r = lax.broadcasted_iota(jnp.int32, (page, page), 0)
    t_in = lax.broadcasted_iota(jnp.int32, (page, page), 1)
    assert spp & (spp - 1) == 0
    stride_of_r = jnp.bitwise_and(r, spp - 1)
    offset_of_r = lax.shift_right_logical(r, int(math.log2(spp)))
    perm = (t_in == stride_of_r * CMP_STRIDE + offset_of_r).astype(BF16)
    for z in range(2):
        for k, pg in enumerate(pages):
            for c in range(cpt):
                slab = pg[0, 0, z, pl.ds(c * LANES, LANES), :].astype(BF16)
                rows_by_offset = lax.dot_general(perm, slab, NT, preferred_element_type=F32)
                for i in range(CMP_STRIDE):
                    xt_scr[z, c, pl.ds(i * n_str + k * spp, spp), :] = rows_by_offset[i * spp:(i + 1) * spp]

    def cached(z):
        return lambda i: jnp.concatenate([xt_scr[z, c, pl.ds(i * n_str, n_str), :] for c in range(cpt)], axis=1)

    kc = _compress(cached(0), w1_ref, w2_ref, b_ref, 0)
    vc = _compress(cached(1), w1_ref, w2_ref, b_ref, 1)

    cb = cb_ref[...]
    s = lax.dot_general(q4, kc.astype(BF16), NT, preferred_element_type=F32) + cb
    m = jnp.max(s, axis=1, keepdims=True)
    e = jnp.where(cb > VIS, jnp.exp(s - m), 0.0)
    den = jnp.sum(e, axis=1, keepdims=True)
    pc = e / jnp.where(den > 0, den, 1.0)
    oc = jnp.dot(pc.astype(BF16), vc.astype(BF16), preferred_element_type=F32)

    imp = _dot_f32(hs_ref[...], _dot_f32(pc, ov_ref[...]))
    nsp = imp.shape[1]
    blk = lax.broadcasted_iota(jnp.int32, (rows, nsp), 1)
    cur = ns - 1
    forced = (blk == 0) | (blk == cur) | (blk == cur - 1)
    score = jnp.where(blk < ns, imp + FORCE_BONUS * forced.astype(F32), LOWEST)
    nsb = -(-ns // 8) * 8
    sel_t = _top_k_mask(score.T[0:nsb], blk.T[0:nsb].astype(F32), 0, min(SEL_TOP_N, ns))
    sel = jnp.concatenate([sel_t, jnp.zeros((nsp - nsb, rows), F32)], axis=0).T.astype(BF16)

    def hide(k):
        return jnp.where(jnp.dot(sel, ex_ref[k], preferred_element_type=F32) > 0.5, 0.0, NEG)

    pad = jnp.zeros((page - tnew, gd), F32)
    tiles = [(pg[0, 0, 2], pg[0, 0, 3], sb_ref[k] + hide(k), True) for k, pg in enumerate(pages)]
    tiles.append((jnp.concatenate([ksn_ref[...], pad], axis=0), jnp.concatenate([vsn_ref[...], pad], axis=0),
                  sb_ref[n_pages] + hide(n_pages), False))
    osel = _attend_rows(tiles, q4, s_scr)

    w_tiles = win_ref.shape[4] // page
    tiles = [(win_ref[0, 0, 0, :, pl.ds(k * page, page)], win_ref[0, 0, 1, :, pl.ds(k * page, page)], wb_ref[k], True)
             for k in range(w_tiles)]
    tiles.append((jnp.concatenate([kwn_ref[...], pad], axis=0), jnp.concatenate([vwn_ref[...], pad], axis=0),
                  wb_ref[w_tiles], False))
    owin = _attend_rows(tiles, q4, s_scr)

    gt = jax.nn.sigmoid(gt_ref[0])
    o = gt[:, 0:1] * oc + gt[:, 1:2] * osel + gt[:, 2:3] * owin
    for gi in range(g):
        o_ref[0, gi] = o[gi * rpg:(gi + 1) * rpg, gi * dh:(gi + 1) * dh].astype(BF16)


def nsa_sample(p, row0, cache_t, layer, win_t, page_table, q4, gates, tables, cweights, *, tnew):
    n, n_pages = page_table.shape
    g, dh = NSA_KV_GROUPS, NSA_HEAD_DIM
    gd = g * dh
    page = cache_t.shape[4]
    rows = q4.shape[1]
    cb, sb, wb, ov, hs, ex = tables
    w1, w2, hid0 = cweights
    ns = -(-(n_pages * page + tnew) // SEL_BLOCK)
    assert row0 % tnew == 0 and page % SEL_BLOCK == 0 and page == LANES and tnew <= page
    assert (n_pages * page + tnew) // CMP_STRIDE == n_pages * page // CMP_STRIDE
    blk0 = row0 // tnew
    q_blocks = (rows // tnew * dh) // gd
    assert gd % LANES == 0
    wbuf = win_t.shape[4]
    assert wbuf % page == 0

    def page_spec(k):
        return pl.BlockSpec((1, 1, NSA_CACHED, gd, page), lambda i, pt, k=k: (layer, pt[i * n_pages + k], 0, 0, 0))

    def new_spec(z):
        return pl.BlockSpec((tnew, gd), lambda i, pt, z=z: (blk0 + i, q_blocks + z))

    def const(shape):
        return pl.BlockSpec(shape, lambda i, pt: (0,) * len(shape))

    grid_spec = pltpu.PrefetchScalarGridSpec(
        num_scalar_prefetch=1,
        grid=(n,),
        in_specs=[page_spec(k) for k in range(n_pages)] + [
            new_spec(2), new_spec(3), new_spec(4), new_spec(5),
            pl.BlockSpec((1, 1, 2, gd, wbuf), lambda i, pt: (layer, i, 0, 0, 0)),
            pl.BlockSpec((1, rows, gd), lambda i, pt: (i, 0, 0)),
            pl.BlockSpec((1, rows, 3), lambda i, pt: (i, 0, 0)),
            const(cb.shape), const(sb.shape), const(wb.shape), const(ov.shape), const(hs.shape), const(ex.shape),
            const(w1.shape), const(w2.shape), const(hid0.shape),
        ],
        out_specs=pl.BlockSpec((1, g, rows // g, dh), lambda i, pt: (i, 0, 0, 0)),
        scratch_shapes=[pltpu.VMEM((n_pages + 1, rows, page), F32),
                        pltpu.VMEM((2, gd // LANES, n_pages * page, LANES), F32)],
    )
    return pl.pallas_call(
        functools.partial(_nsa_sample_kernel, n_pages=n_pages, page=page, tnew=tnew, ns=ns),
        grid_spec=grid_spec,
        out_shape=jax.ShapeDtypeStruct((n, g, rows // g, dh), BF16),
        compiler_params=_params(("arbitrary",)),
        name="nsa_sample",
    )(page_table.reshape(-1), *([cache_t] * n_pages), p, p, p, p, win_t, q4, gates,
      cb, sb, wb, ov, hs, ex, w1, w2, hid0)


def _nsa_sample_tables(rel_bias, past, tnew, wbuf, page, hpg):
    g = NSA_KV_GROUPS
    h = g * hpg
    qpos = past + jnp.arange(tnew)
    n_str = (past + tnew) // CMP_STRIDE
    ns = -(-(past + tnew) // SEL_BLOCK)
    nsp = LANES
    assert ns <= nsp

    def rows(b):
        return b.reshape(h * tnew, b.shape[2])

    def tiled(b, n_tiles):
        return b.reshape(h * tnew, n_tiles, page).transpose(1, 0, 2)

    cmp_end = jnp.arange(n_str) * CMP_STRIDE + CMP_BLOCK - 1
    dc = qpos[:, None] - cmp_end[None, :]
    cb = rows(_dist_bias(rel_bias, dc, (dc >= 0) & (jnp.arange(n_str) < n_str - 1)[None, :]))
    n_tiles = past // page + 1
    kpos = jnp.arange(n_tiles * page)
    ds = qpos[:, None] - kpos[None, :]
    sb = tiled(rows(_dist_bias(rel_bias, ds, (ds >= 0) & (kpos < past + tnew)[None, :])), n_tiles)
    w_tiles = wbuf // page + 1
    r = jnp.arange(w_tiles * page)
    dw = qpos[:, None] - (past - wbuf + r)[None, :]
    wb = tiled(rows(_dist_bias(rel_bias, dw, (dw >= 0) & (dw < WINDOW) & (r < wbuf + tnew)[None, :])), w_tiles)
    ov = jnp.zeros((n_str, nsp), F32).at[:n_str - 1, :ns].set(_cmp_sel_overlap(n_str - 1, ns))
    row = jnp.arange(h * tnew)
    same = (row[:, None] // (hpg * tnew) == row[None, :] // (hpg * tnew)) & \
           (row[:, None] % tnew == row[None, :] % tnew)
    expand = (kpos[None, :] // SEL_BLOCK == jnp.arange(nsp)[:, None]).astype(BF16)
    expand = expand.reshape(nsp, n_tiles, page).transpose(1, 0, 2)
    return cb, sb, wb, ov, same.astype(F32), expand


def _softplus(x):
    return jnp.maximum(x, 0.0) + jnp.log1p(jnp.exp(-jnp.abs(x)))


def _transpose_rows(x):
    r = x.shape[0]
    if r >= GDN_CHUNK:
        return x.T
    return jnp.concatenate([x, jnp.zeros((LANES - r, x.shape[1]), x.dtype)], axis=0).T[:, 0:r]


def _unit_lower_inverse(a):
    c = a[0].shape[0]
    eye = (lax.broadcasted_iota(jnp.int32, (c, c), 0) == lax.broadcasted_iota(jnp.int32, (c, c), 1)).astype(F32)
    inv = [eye - x for x in a]
    power = a
    n = 2
    while n < c:
        power = [_dot(x, x) for x in power]
        inv = [y + _dot(y, x) for x, y in zip(power, inv)]
        n *= 2
    resid = [(eye - y) - _dot_split(x, y) for x, y in zip(a, inv)]
    return [y + _dot(y, r) for y, r in zip(inv, resid)]


def _gdn_kernel(qkv_ref, z_ref, ab_ref, hist_ref, s0_ref, cw_ref, nar_ref, dtr_ref, nac_ref, dtc_ref,
                nw_ref, o_ref, sfin_ref, s_scr, carry_scr, *, c, tv, qk_heads, v_heads):
    hd = GDN_HEAD
    halo = carry_scr.shape[0]
    step = pl.program_id(1)

    @pl.when(step == 0)
    def _():
        s_scr[...] = s0_ref[0, 0]
        carry_scr[...] = hist_ref[0]

    x = qkv_ref[...]
    ext = jnp.concatenate([carry_scr[...], x], axis=0)
    off = halo - (GDN_CONV - 1)
    conv = ext[off:off + tv] * cw_ref[0:1]
    for i in range(1, GDN_CONV):
        conv = conv + ext[off + i:off + i + tv] * cw_ref[i:i + 1]
    carry_scr[...] = ext[tv:tv + halo]
    act = conv * jax.nn.sigmoid(conv)
    if tv < c:
        act = jnp.concatenate([act, jnp.zeros((c - tv, act.shape[1]), F32)], axis=0)
        ab = jnp.concatenate([ab_ref[...], jnp.zeros((c - tv, ab_ref.shape[1]), F32)], axis=0)
    else:
        ab = ab_ref[...]

    row = lax.broadcasted_iota(jnp.int32, (c, c), 0)
    col = lax.broadcasted_iota(jnp.int32, (c, c), 1)
    incl = row >= col
    strict = row > col
    lower = incl.astype(F32)
    upper = (row <= col).astype(F32)

    live_c = lax.broadcasted_iota(jnp.int32, ab.shape, 0) < tv
    g_col = jnp.where(live_c, nar_ref[...] * _softplus(ab + dtr_ref[...]), 0.0)
    beta = jnp.where(live_c, jax.nn.sigmoid(ab), 0.0)
    abt = _transpose_rows(ab)[0:v_heads]
    live_r = lax.broadcasted_iota(jnp.int32, abt.shape, 1) < tv
    g_row = jnp.where(live_r, nac_ref[...] * _softplus(abt + dtc_ref[...]), 0.0)
    gam_col = _dot_f32(lower, g_col)
    gam_row = _dot_f32(g_row, upper)

    rep = v_heads // qk_heads
    hvs = range(v_heads)
    qn, kn, kk, qk = [], [], [], []
    for hq in range(qk_heads):
        qh = act[:, hq * hd:(hq + 1) * hd]
        kh = act[:, (qk_heads + hq) * hd:(qk_heads + hq + 1) * hd]
        qn.append(qh * lax.rsqrt(jnp.sum(qh * qh, axis=-1, keepdims=True) + 1e-6) * (hd ** -0.5))
        kn.append(kh * lax.rsqrt(jnp.sum(kh * kh, axis=-1, keepdims=True) + 1e-6))
    for hq in range(qk_heads):
        kk.append(_dot_nt(kn[hq], kn[hq]))
        qk.append(_dot_nt(qn[hq], kn[hq]))
    gc = [gam_col[:, hv:hv + 1] for hv in hvs]
    bc = [beta[:, v_heads + hv:v_heads + hv + 1] for hv in hvs]
    decay = [jnp.where(incl, jnp.exp(jnp.where(incl, gc[hv] - gam_row[hv:hv + 1, :], 0.0)), 0.0) for hv in hvs]
    a_mat = [jnp.where(strict, bc[hv] * kk[hv // rep] * decay[hv], 0.0) for hv in hvs]
    tinv = _unit_lower_inverse(a_mat)
    eg = [jnp.exp(gc[hv]) for hv in hvs]
    g_last = [gc[hv][c - 1:c, :] for hv in hvs]
    rhs = [jnp.concatenate([act[:, (2 * qk_heads + hv) * hd:(2 * qk_heads + hv + 1) * hd] * bc[hv],
                            kn[hv // rep] * (bc[hv] * eg[hv])], axis=1) for hv in hvs]
    vk = [_dot(tinv[hv], rhs[hv]) for hv in hvs]
    s_old = [s_scr[hv] for hv in hvs]
    ks = [_dot(jnp.concatenate([vk[hv][:, hd:], qn[hv // rep] * eg[hv]], axis=0), s_old[hv]) for hv in hvs]
    u = [vk[hv][:, :hd] - ks[hv][:c] for hv in hvs]
    k_dec = [_transpose_rows(kn[hv // rep] * jnp.exp(g_last[hv] - gc[hv])) for hv in hvs]
    upd = [_dot(jnp.concatenate([qk[hv // rep] * decay[hv], k_dec[hv]], axis=0), u[hv]) for hv in hvs]
    for hv in hvs:
        s_scr[hv] = s_old[hv] * jnp.exp(g_last[hv]) + upd[hv][c:]
        o = (ks[hv][c:] + upd[hv][:c])[0:tv]
        zh = z_ref[:, hv * hd:(hv + 1) * hd]
        o = _rms(o, nw_ref[...]) * (zh * jax.nn.sigmoid(zh))
        o_ref[:, hv * hd:(hv + 1) * hd] = o.astype(BF16)

    @pl.when(step == pl.num_programs(1) - 1)
    def _():
        sfin_ref[0] = s_scr[...]


def gdn(p, row0, n, t, hist, s0, layer, conv_w, a_log, dt_bias, norm_w, *, qk_heads, v_heads):
    hd = GDN_HEAD
    ch = (2 * qk_heads + v_heads) * hd
    vw = v_heads * hd
    tv = min(GDN_CHUNK, t)
    c = max(tv, 16)
    assert t % tv == 0 and row0 % tv == 0 and ch % vw == 0
    nchunk = t // tv
    blk0 = row0 // tv
    halo = 8
    histp = jnp.concatenate([jnp.zeros((n, halo - (GDN_CONV - 1), ch), F32), hist.astype(F32)], axis=1)
    cw = jnp.concatenate([conv_w.astype(F32), jnp.zeros((halo - GDN_CONV, ch), F32)], axis=0)
    neg_a = -jnp.exp(a_log.astype(F32))
    dt = dt_bias.astype(F32)
    pad = jnp.zeros((LANES - v_heads,), F32)
    nar = jnp.concatenate([neg_a, pad]).reshape(1, LANES)
    dtr = jnp.concatenate([dt, pad]).reshape(1, LANES)
    o, sfin = pl.pallas_call(
        functools.partial(_gdn_kernel, c=c, tv=tv, qk_heads=qk_heads, v_heads=v_heads),
        grid=(n, nchunk),
        in_specs=[
            pl.BlockSpec((tv, ch), lambda b, s: (blk0 + b * nchunk + s, 0)),
            pl.BlockSpec((tv, vw), lambda b, s: (blk0 + b * nchunk + s, ch // vw)),
            pl.BlockSpec((tv, LANES), lambda b, s: (blk0 + b * nchunk + s, (ch + vw) // LANES)),
            pl.BlockSpec((1, halo, ch), lambda b, s: (b, 0, 0)),
            pl.BlockSpec((1, 1, v_heads, hd, hd), lambda b, s: (layer, b, 0, 0, 0)),
            pl.BlockSpec((halo, ch), lambda b, s: (0, 0)),
            pl.BlockSpec((1, LANES), lambda b, s: (0, 0)),
            pl.BlockSpec((1, LANES), lambda b, s: (0, 0)),
            pl.BlockSpec((v_heads, 1), lambda b, s: (0, 0)),
            pl.BlockSpec((v_heads, 1), lambda b, s: (0, 0)),
            pl.BlockSpec((1, hd), lambda b, s: (0, 0)),
        ],
        out_specs=[
            pl.BlockSpec((tv, vw), lambda b, s: (b * nchunk + s, 0)),
            pl.BlockSpec((1, v_heads, hd, hd), lambda b, s: (b, 0, 0, 0)),
        ],
        out_shape=[jax.ShapeDtypeStruct((n * t, vw), BF16),
                   jax.ShapeDtypeStruct((n, v_heads, hd, hd), F32)],
        scratch_shapes=[pltpu.VMEM((v_heads, hd, hd), F32), pltpu.VMEM((halo, ch), F32)],
        compiler_params=_params(("parallel", "arbitrary")),
        name="gdn",
    )(p, p, p, histp, s0.astype(F32), cw, nar, dtr, neg_a.reshape(v_heads, 1), dt.reshape(v_heads, 1),
      norm_w.astype(F32).reshape(1, hd))
    return o, sfin


def _pow2_divisor(m, cap):
    tile = cap
    while m % tile:
        tile //= 2
    return tile


def _window(p, r0, r1, c0, c1):
    return lax.slice(p, (r0, c0), (r1, c1))


def _pad_cols(w, mult):
    n = w.shape[-1]
    return jnp.pad(w, ((0, 0), (0, -n % mult)))


def kernel(x_prompt, x_sample, cache_kv, cache_win, state_ssm, state_conv, page_table, rel_bias, norm_mix, norm_ffn, norm_final, nsa_w_in, nsa_w_out, nsa_cmp_w1, nsa_cmp_w2, nsa_cmp_pe, gdn_w_in, gdn_conv_w, gdn_a_log, gdn_dt_bias, gdn_norm_w, gdn_w_out, ffn_w1, ffn_w2):
    nb, t, d = x_prompt.shape
    ns_, tnew, _ = x_sample.shape
    depth = norm_mix.shape[0]
    g, dh = NSA_KV_GROUPS, NSA_HEAD_DIM
    gd = g * dh
    heads = nsa_w_out.shape[1] // dh
    hpg = heads // g
    qw = heads * dh
    mp = nb * t
    past = page_table.shape[1] * cache_kv.shape[2]
    wbuf = cache_win.shape[2]
    page = cache_kv.shape[2]
    v_heads = gdn_a_log.shape[1]
    qk_heads = (gdn_conv_w.shape[2] // GDN_HEAD - v_heads) // 2
    conv_ch = gdn_conv_w.shape[2]
    m_all = nb * t + ns_ * tnew
    tn_nsa = -(-nsa_w_in.shape[2] // LANES) * LANES
    tn_gdn = -(-gdn_w_in.shape[2] // (2 * LANES)) * LANES
    tm_nsa = _pow2_divisor(m_all, 1024)
    tm_gdn = _pow2_divisor(m_all, 512)
    tm_ffn = _pow2_divisor(math.gcd(mp, ns_ * tnew), 512)

    x = jnp.concatenate([x_prompt.reshape(mp, d), x_sample.reshape(ns_ * tnew, d)], axis=0).astype(F32)
    tq = LANES
    p_tables = _nsa_prompt_t_tables(rel_bias, t, tq, hpg)
    s_tables = _nsa_sample_tables(rel_bias, past, tnew, wbuf, page, hpg)
    cache_t = cache_kv.transpose(0, 1, 3, 4, 5, 2).reshape(cache_kv.shape[0], cache_kv.shape[1], NSA_CACHED, gd, page)
    win_t = cache_win.transpose(0, 1, 3, 4, 5, 2).reshape(cache_win.shape[0], ns_, 2, gd, wbuf)

    kv_p, kv_s, win_p, win_s, ssm_p, ssm_s, conv_p, conv_s = [], [], [], [], [], [], [], []
    for i in range(depth):
        li = i // 2
        if i % 2 == 0:
            w_in = _pad_cols(nsa_w_in[li], tn_nsa).astype(BF16)
            p = norm_matmul(x, norm_mix[i], w_in, tm=tm_nsa, tn=tn_nsa)
            kv_end = qw + NSA_N_KV * gd
            ps = _window(p, mp, m_all, 0, kv_end + 3 * heads).reshape(ns_, tnew, -1)
            cweights = _compress_weights(nsa_cmp_w1[li], nsa_cmp_w2[li], nsa_cmp_pe[li])
            kc, vct = nsa_compress(p, nb, t, qw, *cweights)
            kv4 = _window(p, 0, mp, qw + 2 * gd, kv_end).reshape(nb, t, 2, 2, g, dh)
            kt = kv4[:, :, :, 0].transpose(2, 0, 3, 1, 4).astype(BF16)
            vt = kv4[:, :, :, 1].transpose(2, 0, 3, 4, 1).astype(BF16)
            aug = jnp.zeros((2, nb, g, V_AUG, t), BF16).at[:, :, :, 0].set(1.0)
            vt = jnp.concatenate([vt, aug], axis=3).reshape(2, nb, g, dh + V_AUG, t // tq, tq)
            vt = vt.transpose(0, 1, 2, 4, 3, 5)
            gates_p = _window(p, 0, mp, kv_end, kv_end + 3 * heads).reshape(mp, g, 3 * hpg).transpose(1, 2, 0)
            o_p = nsa_prompt_t(p, gates_p, kc, vct, kt, vt, *p_tables, nb, t, tq=tq, sub=4)
            q_s = (ps[:, :, :qw] * dh ** -0.5).reshape(ns_, tnew, g, hpg, dh)
            q4 = jnp.einsum("ntghd,gk->nghtkd", q_s, jnp.eye(g, dtype=F32))
            q4 = q4.reshape(ns_, heads * tnew, gd).astype(BF16)
            gates_s = ps[:, :, kv_end:kv_end + 3 * heads].reshape(ns_, tnew, heads, 3)
            gates_s = gates_s.transpose(0, 2, 1, 3).reshape(ns_, heads * tnew, 3)
            o_s = nsa_sample(p, mp, cache_t, li, win_t, page_table, q4, gates_s, s_tables, cweights, tnew=tnew)
            o_s = o_s.reshape(ns_, g, hpg, tnew, dh).transpose(0, 3, 1, 2, 4).reshape(ns_ * tnew, qw)
            w_out = nsa_w_out[li].astype(BF16)
            kv_p.append(_window(p, 0, mp, qw, qw + NSA_CACHED * gd).reshape(nb, t, NSA_CACHED, g, dh))
            kv_s.append(ps[:, :, qw:qw + NSA_CACHED * gd].reshape(ns_, tnew, NSA_CACHED, g, dh))
            wkeep = min(WINDOW, t)
            win_p.append(jnp.stack([_window(p, b * t + t - wkeep, (b + 1) * t, qw + NSA_CACHED * gd, kv_end)
                                    for b in range(nb)]).reshape(nb, wkeep, 2, g, dh))
            win_s.append(ps[:, :, qw + NSA_CACHED * gd:kv_end].reshape(ns_, tnew, 2, g, dh))
        else:
            w_in = _pad_cols(gdn_w_in[li], tn_gdn).astype(BF16)
            p = norm_matmul(x, norm_mix[i], w_in, tm=tm_gdn, tn=tn_gdn)
            args = (gdn_conv_w[li], gdn_a_log[li], gdn_dt_bias[li], gdn_norm_w[li])
            h0 = jnp.zeros((nb, GDN_CONV - 1, conv_ch), F32)
            s0 = jnp.zeros((1, nb, v_heads, GDN_HEAD, GDN_HEAD), F32)
            o_p, sp = gdn(p, 0, nb, t, h0, s0, 0, *args, qk_heads=qk_heads, v_heads=v_heads)
            o_s, ss = gdn(p, mp, ns_, tnew, state_conv[li], state_ssm, li, *args,
                          qk_heads=qk_heads, v_heads=v_heads)
            w_out = gdn_w_out[li].astype(BF16)
            ssm_p.append(sp.astype(state_ssm.dtype))
            ssm_s.append(ss.astype(state_ssm.dtype))
            keep = GDN_CONV - 1
            assert t >= keep and tnew >= keep
            conv_p.append(jnp.stack([_window(p, (b + 1) * t - keep, (b + 1) * t, 0, conv_ch) for b in range(nb)]))
            conv_s.append(_window(p, mp, m_all, 0, conv_ch).reshape(ns_, tnew, conv_ch)[:, tnew - keep:])
        x = mix_ffn(x, o_p, o_s, w_out, norm_ffn[i], ffn_w1[i].astype(BF16), ffn_w2[i].astype(BF16), norm_final,
                    final_norm=(i == depth - 1), tm=tm_ffn, th=1024)
    y_prompt = x[:mp].reshape(nb, t, d)
    y_sample = x[mp:].reshape(ns_, tnew, d)
    new_win = jnp.stack(win_s).astype(cache_win.dtype)
    win_s_all = jnp.concatenate([cache_win[:new_win.shape[0]], new_win], axis=2)[:, :, -wbuf:]
    return (y_prompt, y_sample, jnp.stack(kv_p), jnp.stack(kv_s), jnp.stack(win_p), win_s_all,
            jnp.stack(ssm_p), jnp.stack(ssm_s), jnp.stack(conv_p), jnp.stack(conv_s))
```

```python
import functools
import math

import jax
import jax.numpy as jnp
from jax import lax
from jax.experimental import pallas as pl
from jax.experimental.pallas import tpu as pltpu

F32 = jnp.float32
BF16 = jnp.bfloat16
HIGHEST = lax.Precision.HIGHEST

LANES = 128
VMEM_LIMIT = 56 * 1024 * 1024

NSA_HEAD_DIM = 64
NSA_KV_GROUPS = 4
NSA_N_KV = 6
NSA_CACHED = 4
CMP_BLOCK = 32
CMP_STRIDE = 16
SEL_BLOCK = 64
SEL_TOP_N = 8
WINDOW = 512
FORCE_BONUS = 1e4
REL_BUCKETS = 32
REL_MAX_DIST = 128
GDN_HEAD = 128
GDN_CONV = 4
GDN_CHUNK = 64
RMS_EPS = 1e-6
NEG = -1e30
VIS = -1e29
LOWEST = -3e38

NT = (((1,), (1,)), ((), ()))


def _params(sem):
    return pltpu.CompilerParams(dimension_semantics=sem, vmem_limit_bytes=VMEM_LIMIT)


def _dot(a, b):
    return jnp.dot(a.astype(BF16), b.astype(BF16), preferred_element_type=F32)


def _dot_nt(a, b):
    return lax.dot_general(a.astype(BF16), b.astype(BF16), NT, preferred_element_type=F32)


def _dot_split(a, b):
    ah, bh = a.astype(BF16), b.astype(BF16)
    al = (a - ah.astype(F32)).astype(BF16)
    bl = (b - bh.astype(F32)).astype(BF16)
    hh = jnp.dot(ah, bh, preferred_element_type=F32)
    return hh + (jnp.dot(ah, bl, preferred_element_type=F32) + jnp.dot(al, bh, preferred_element_type=F32))


def _dot_f32(a, b):
    return jnp.dot(a, b, precision=HIGHEST, preferred_element_type=F32)


def _rms(x, gain):
    ms = jnp.mean(x * x, axis=-1, keepdims=True)
    return x * lax.rsqrt(ms + RMS_EPS) * gain


def _norm_matmul_kernel(x_ref, g_ref, w_ref, o_ref, xn_ref):
    @pl.when(pl.program_id(1) == 0)
    def _():
        xn_ref[...] = _rms(x_ref[...], g_ref[...]).astype(BF16)

    o_ref[...] = jnp.dot(xn_ref[...], w_ref[...], preferred_element_type=F32)


def norm_matmul(x, gain, w, *, tm, tn):
    m, d = x.shape
    n = w.shape[1]
    assert m % tm == 0 and n % tn == 0
    return pl.pallas_call(
        _norm_matmul_kernel,
        grid=(m // tm, n // tn),
        in_specs=[
            pl.BlockSpec((tm, d), lambda i, j: (i, 0)),
            pl.BlockSpec((1, d), lambda i, j: (0, 0)),
            pl.BlockSpec((d, tn), lambda i, j: (0, j)),
        ],
        out_specs=pl.BlockSpec((tm, tn), lambda i, j: (i, j)),
        out_shape=jax.ShapeDtypeStruct((m, n), F32),
        scratch_shapes=[pltpu.VMEM((tm, d), BF16)],
        compiler_params=_params(("parallel", "arbitrary")),
        name="norm_matmul",
    )(x, gain.reshape(1, d), w)


def _mix_ffn_kernel(x_ref, oa_ref, ob_ref, wo_ref, g_ref, w1_ref, w2_ref, gf_ref, y_ref,
                    x1_ref, xn_ref, acc_ref, *, final_norm, tiles_a):
    j = pl.program_id(1)

    @pl.when(j == 0)
    def _():
        o = jnp.where(pl.program_id(0) < tiles_a, oa_ref[...], ob_ref[...])
        x1 = x_ref[...] + jnp.dot(o, wo_ref[...], preferred_element_type=F32)
        x1_ref[...] = x1
        xn_ref[...] = _rms(x1, g_ref[...]).astype(BF16)
        acc_ref[...] = jnp.zeros_like(acc_ref)

    hid = jnp.maximum(jnp.dot(xn_ref[...], w1_ref[...], preferred_element_type=F32), 0.0)
    acc_ref[...] += jnp.dot((hid * hid).astype(BF16), w2_ref[...], preferred_element_type=F32)

    @pl.when(j == pl.num_programs(1) - 1)
    def _():
        y = x1_ref[...] + acc_ref[...]
        if final_norm:
            y = _rms(y, gf_ref[...])
        y_ref[...] = y


def mix_ffn(x, o_a, o_b, w_out, gain, w1, w2, gain_final, *, final_norm, tm, th):
    m, d = x.shape
    ko = o_a.shape[1]
    hdim = w1.shape[1]
    assert m % tm == 0 and hdim % th == 0 and o_a.shape[0] % tm == 0 and o_a.shape[0] + o_b.shape[0] == m
    tiles_a = o_a.shape[0] // tm
    return pl.pallas_call(
        functools.partial(_mix_ffn_kernel, final_norm=final_norm, tiles_a=tiles_a),
        grid=(m // tm, hdim // th),
        in_specs=[
            pl.BlockSpec((tm, d), lambda i, j: (i, 0)),
            pl.BlockSpec((tm, ko), lambda i, j: (jnp.minimum(i, tiles_a - 1), 0)),
            pl.BlockSpec((tm, ko), lambda i, j: (jnp.maximum(i - tiles_a, 0), 0)),
            pl.BlockSpec((ko, d), lambda i, j: (0, 0)),
            pl.BlockSpec((1, d), lambda i, j: (0, 0)),
            pl.BlockSpec((d, th), lambda i, j: (0, j)),
            pl.BlockSpec((th, d), lambda i, j: (j, 0)),
            pl.BlockSpec((1, d), lambda i, j: (0, 0)),
        ],
        out_specs=pl.BlockSpec((tm, d), lambda i, j: (i, 0)),
        out_shape=jax.ShapeDtypeStruct((m, d), F32),
        scratch_shapes=[pltpu.VMEM((tm, d), F32), pltpu.VMEM((tm, d), BF16), pltpu.VMEM((tm, d), F32)],
        compiler_params=_params(("parallel", "arbitrary")),
        name="mix_ffn",
    )(x, o_a, o_b, w_out, gain.reshape(1, d), w1, w2, gain_final.reshape(1, d))


def _rel_bucket(dist):
    n = jnp.maximum(dist, 0)
    max_exact = REL_BUCKETS // 2
    nf = jnp.maximum(n, 1).astype(F32)
    large = max_exact + (jnp.log(nf / max_exact) / math.log(REL_MAX_DIST / max_exact)
                         * (REL_BUCKETS - max_exact)).astype(jnp.int32)
    return jnp.where(n < max_exact, n, jnp.minimum(large, REL_BUCKETS - 1))


def _dist_bias(rel_bias, dist, visible):
    tbl = rel_bias.astype(F32)
    onehot = (_rel_bucket(dist)[..., None] == jnp.arange(REL_BUCKETS)).astype(F32)
    b = jnp.einsum("...k,kh->h...", onehot, tbl, precision=HIGHEST)
    return jnp.where(visible[None], b, NEG)


def _cmp_sel_overlap(nc, ns):
    c0 = jnp.arange(nc) * CMP_STRIDE
    s0 = jnp.arange(ns) * SEL_BLOCK
    ov = jnp.minimum(c0[:, None] + CMP_BLOCK, s0[None, :] + SEL_BLOCK) - jnp.maximum(c0[:, None], s0[None, :])
    return jnp.maximum(ov, 0).astype(F32) / CMP_BLOCK


def _compress_weights(cw1, cw2, cpe):
    g = NSA_KV_GROUPS
    eye = jnp.eye(g, dtype=F32)
    w1 = jnp.einsum("zide,gh->zigdhe", cw1.astype(F32), eye)
    w1 = w1.reshape(2, CMP_BLOCK, g * NSA_HEAD_DIM, g * NSA_HEAD_DIM).astype(BF16)
    w2 = jnp.einsum("zde,gh->zgdhe", cw2.astype(F32), eye)
    w2 = w2.reshape(2, g * NSA_HEAD_DIM, g * NSA_HEAD_DIM).astype(BF16)
    hid0 = jnp.einsum("zid,zide->ze", cpe.astype(F32), cw1.astype(F32), precision=HIGHEST)
    hid0 = jnp.tile(hid0[:, None, :], (1, 1, g))
    return w1, w2, hid0


def _compress(load_rows, w1_ref, w2_ref, b_ref):
    r = CMP_BLOCK // CMP_STRIDE
    assert r == 2
    types = range(len(load_rows))
    acc_a = [None for _ in types]
    acc_b = [None for _ in types]
    for i in range(CMP_STRIDE):
        for z in types:
            xi = load_rows[z](i).astype(BF16)
            da = jnp.dot(xi, w1_ref[z, i], preferred_element_type=F32)
            db = jnp.dot(xi, w1_ref[z, CMP_STRIDE + i], preferred_element_type=F32)
            acc_a[z] = da if acc_a[z] is None else acc_a[z] + da
            acc_b[z] = db if acc_b[z] is None else acc_b[z] + db
    n_str = acc_a[0].shape[0]
    out = []
    for z in types:
        hid = acc_a[z] + pltpu.roll(acc_b[z], n_str - 1, axis=0) + b_ref[z]
        act = hid * jax.nn.sigmoid(hid)
        out.append(jnp.dot(act.astype(BF16), w2_ref[z], preferred_element_type=F32))
    return out


def _top_k_mask(score, index, axis, k):
    sel = jnp.zeros(score.shape, F32)
    for _ in range(k):
        mx = jnp.max(score, axis=axis, keepdims=True)
        cand = jnp.where(score == mx, index, 1e9)
        first = jnp.min(cand, axis=axis, keepdims=True)
        hit = index == first
        sel = jnp.where(hit, 1.0, sel)
        score = jnp.where(hit, LOWEST, score)
    return sel


def _nsa_compress_kernel(*refs, n_str, n_half):
    g, dh = NSA_KV_GROUPS, NSA_HEAD_DIM
    srcs = (refs[:n_half], refs[n_half:2 * n_half])
    w1_ref, w2_ref, b_ref, kc_ref, vc_ref = refs[2 * n_half:]
    def load_rows(z):
        return lambda i: jnp.concatenate([r[pl.ds(i, n_str, stride=CMP_STRIDE), :] for r in srcs[z]], axis=1)

    kc, vc = _compress([load_rows(0), load_rows(1)], w1_ref, w2_ref, b_ref)
    vct = vc.T
    for gi in range(g):
        kc_ref[0, gi] = kc[:, gi * dh:(gi + 1) * dh].astype(BF16)
        vc_ref[0, gi] = vct[gi * dh:(gi + 1) * dh].astype(BF16)


def nsa_compress(p, n, t, qw, w1, w2, hid0):
    gd = NSA_KV_GROUPS * NSA_HEAD_DIM
    n_str = t // CMP_STRIDE
    assert qw % LANES == 0 and gd % LANES == 0
    n_half = gd // LANES
    q_blocks = qw // LANES
    out = [jax.ShapeDtypeStruct((n, NSA_KV_GROUPS, n_str, NSA_HEAD_DIM), BF16),
           jax.ShapeDtypeStruct((n, NSA_KV_GROUPS, NSA_HEAD_DIM, n_str), BF16)]
    return pl.pallas_call(
        functools.partial(_nsa_compress_kernel, n_str=n_str, n_half=n_half),
        grid=(n,),
        in_specs=[pl.BlockSpec((t, LANES), lambda i, c=c: (i, q_blocks + c)) for c in range(2 * n_half)] + [
            pl.BlockSpec(w1.shape, lambda i: (0, 0, 0, 0)),
            pl.BlockSpec(w2.shape, lambda i: (0, 0, 0)),
            pl.BlockSpec(hid0.shape, lambda i: (0, 0, 0)),
        ],
        out_specs=[pl.BlockSpec((1, NSA_KV_GROUPS, n_str, NSA_HEAD_DIM), lambda i: (i, 0, 0, 0)),
                   pl.BlockSpec((1, NSA_KV_GROUPS, NSA_HEAD_DIM, n_str), lambda i: (i, 0, 0, 0))],
        out_shape=out,
        compiler_params=_params(("parallel",)),
        name="nsa_compress",
    )(*([p] * (2 * n_half)), w1, w2, hid0)


V_AUG = 16


def _softmax_tile_t(carry, s, vt):
    m, acc = carry
    m_new = jnp.maximum(m, jnp.max(s, axis=0, keepdims=True))
    p = jnp.exp(s - m_new).astype(BF16)
    acc = jnp.exp(m - m_new) * acc + jnp.dot(vt, p, preferred_element_type=F32)
    return m_new, acc


def _nsa_prompt_t_kernel(q_ref, gt_ref, kc_ref, vct_ref, ks_ref, vst_ref, kw_ref, vwt_ref,
                         cb_ref, bd_ref, ovt_ref, o_ref, hide_scr, *, tq, hpg, sub):
    dh = NSA_HEAD_DIM
    qi = pl.program_id(2)
    nsel = ovt_ref.shape[0]
    lanes = hpg * tq
    q = q_ref[...] * (dh ** -0.5)
    q4 = jnp.concatenate([q[:, h * dh:(h + 1) * dh] for h in range(hpg)], axis=0).astype(BF16)

    def logits(k):
        return lax.dot_general(k, q4, NT, preferred_element_type=F32)

    cb = cb_ref[0, 0]
    s = logits(kc_ref[0, 0]) + cb
    m = jnp.max(s, axis=0, keepdims=True)
    e = jnp.where(cb > VIS, jnp.exp(s - m), 0.0)
    den = jnp.sum(e, axis=0, keepdims=True)
    pc = e / jnp.where(den > 0, den, 1.0)
    oc = jnp.dot(vct_ref[0, 0], pc.astype(BF16), preferred_element_type=F32)

    pcs = pc[:, 0:tq]
    for h in range(1, hpg):
        pcs = pcs + pc[:, h * tq:(h + 1) * tq]
    imp = _dot_f32(ovt_ref[...], pcs)
    blk = lax.broadcasted_iota(jnp.int32, (nsel, tq), 0)
    tpos = qi * tq + lax.broadcasted_iota(jnp.int32, (nsel, tq), 1)
    cur = lax.shift_right_logical(tpos, int(math.log2(SEL_BLOCK)))
    forced = (blk == 0) | (blk == cur) | (blk == cur - 1)
    score = jnp.where(blk * SEL_BLOCK <= tpos, imp + FORCE_BONUS * forced.astype(F32), -1e9)
    sel = _top_k_mask(score, blk.astype(F32), 0, min(SEL_TOP_N, nsel))
    hide_scr[...] = jnp.where(sel > 0.5, 0.0, NEG)

    per_sub = tq // SEL_BLOCK
    init = (jnp.full((1, lanes), NEG, F32), jnp.zeros((dh + V_AUG, lanes), F32))

    n_back = WINDOW // tq
    ks, vs, adds = [], [], []
    for dt in range(n_back, -1, -1):
        j = qi - dt
        jc = jnp.maximum(j, 0)
        tile = 0 if dt == 0 else 1 if dt == 1 else 2 if dt == n_back else 3
        ks.append(kw_ref[0, 0, 0, pl.ds(pl.multiple_of(jc * tq, tq), tq), :])
        vs.append(vwt_ref[0, 0, 0, jc])
        adds.append(bd_ref[0, tile] + jnp.where(j >= 0, 0.0, NEG))
    s = logits(jnp.concatenate(ks, axis=0)) + jnp.concatenate(adds, axis=0)
    _, acc = _softmax_tile_t(init, s, jnp.concatenate(vs, axis=1))
    l = acc[dh:dh + 1]
    owin = acc[0:dh] / jnp.where(l > 0, l, 1.0)

    def rel_tile(rel):
        return jnp.where(rel < 0, 4, jnp.where(rel == 0, 0, jnp.where(rel == 1, 1, 3)))

    def sel_tile(j, carry):
        ks, vs, adds = [], [], []
        for s_i in range(sub):
            js = j * sub + s_i
            ks.append(ks_ref[0, 0, 0, pl.ds(pl.multiple_of(js * tq, tq), tq), :])
            vs.append(vst_ref[0, 0, 0, js])
            hide = jnp.concatenate(
                [jnp.broadcast_to(hide_scr[pl.ds(js * per_sub + b, 1), :], (SEL_BLOCK, tq)) for b in range(per_sub)],
                axis=0)
            adds.append(bd_ref[0, rel_tile(qi - js)] + jnp.concatenate([hide] * hpg, axis=1))
        s = logits(jnp.concatenate(ks, axis=0)) + jnp.concatenate(adds, axis=0)
        return _softmax_tile_t(carry, s, jnp.concatenate(vs, axis=1))

    jd = qi // sub
    carry = sel_tile(jd, init)
    _, acc = lax.fori_loop(0, jd, sel_tile, carry)
    l = acc[dh:dh + 1]
    osel = acc[0:dh] / jnp.where(l > 0, l, 1.0)

    gt = jax.nn.sigmoid(gt_ref[0])
    outs = []
    for h in range(hpg):
        blk_h = slice(h * tq, (h + 1) * tq)
        outs.append(gt[3 * h:3 * h + 1] * oc[:, blk_h] + gt[3 * h + 1:3 * h + 2] * osel[:, blk_h]
                    + gt[3 * h + 2:3 * h + 3] * owin[:, blk_h])
    o_ref[...] = jnp.concatenate(outs, axis=0).T.astype(BF16)


def nsa_prompt_t(p, gates_t, kc, vct, kt, vt, cmp_bias, diag_bias, ovt, n, t, *, tq, sub):
    g, dh = NSA_KV_GROUPS, NSA_HEAD_DIM
    nq = t // tq
    hpg = cmp_bias.shape[3] // tq
    nc = kc.shape[2]
    nsel = ovt.shape[0]
    assert t % (tq * sub) == 0 and WINDOW % tq == 0 and tq % SEL_BLOCK == 0

    def k_spec(z):
        return pl.BlockSpec((1, 1, 1, t, dh), lambda b, gi, qi, z=z: (z, b, gi, 0, 0))

    def vt_spec(z):
        return pl.BlockSpec((1, 1, 1, nq, dh + V_AUG, tq), lambda b, gi, qi, z=z: (z, b, gi, 0, 0, 0))

    return pl.pallas_call(
        functools.partial(_nsa_prompt_t_kernel, tq=tq, hpg=hpg, sub=sub),
        grid=(n, g, nq),
        in_specs=[
            pl.BlockSpec((tq, hpg * dh), lambda b, gi, qi: (b * nq + qi, gi)),
            pl.BlockSpec((1, 3 * hpg, tq), lambda b, gi, qi: (gi, 0, b * nq + qi)),
            pl.BlockSpec((1, 1, nc, dh), lambda b, gi, qi: (b, gi, 0, 0)),
            pl.BlockSpec((1, 1, dh, nc), lambda b, gi, qi: (b, gi, 0, 0)),
            k_spec(0), vt_spec(0), k_spec(1), vt_spec(1),
            pl.BlockSpec((1, 1, nc, hpg * tq), lambda b, gi, qi: (gi, qi, 0, 0)),
            pl.BlockSpec((1, 5, tq, hpg * tq), lambda b, gi, qi: (gi, 0, 0, 0)),
            pl.BlockSpec(ovt.shape, lambda b, gi, qi: (0, 0)),
        ],
        out_specs=pl.BlockSpec((tq, hpg * dh), lambda b, gi, qi: (b * nq + qi, gi)),
        out_shape=jax.ShapeDtypeStruct((n * t, g * hpg * dh), BF16),
        scratch_shapes=[pltpu.VMEM((nsel, tq), F32)],
        compiler_params=_params(("parallel", "parallel", "arbitrary")),
        name="nsa_prompt",
    )(p, gates_t, kc, vct, kt, vt, kt, vt, cmp_bias, diag_bias, ovt)


def _nsa_prompt_t_tables(rel_bias, t, tq, hpg):
    g = NSA_KV_GROUPS
    nc = t // CMP_STRIDE
    ns = -(-t // SEL_BLOCK)
    nq = t // tq

    def lanes(b):
        k = b.shape[1]
        return b.reshape(g, hpg, k, tq).transpose(0, 2, 1, 3).reshape(g, k, hpg * tq)

    tpos = jnp.arange(t)
    cmp_end = jnp.arange(nc) * CMP_STRIDE + CMP_BLOCK - 1
    dc = tpos[None, :] - cmp_end[:, None]
    cmp_bias = _dist_bias(rel_bias, dc, (dc >= 0) & (jnp.arange(nc) < nc - 1)[:, None])
    cmp_bias = cmp_bias.reshape(g, hpg, nc, nq, tq).transpose(0, 3, 2, 1, 4).reshape(g, nq, nc, hpg * tq)
    key = jnp.arange(tq)[:, None]
    qry = jnp.arange(tq)[None, :]
    d0 = qry - key
    d1 = tq + qry - key
    dw = WINDOW + qry - key
    far = jnp.full((tq, tq), 2 * REL_MAX_DIST)
    assert tq >= REL_MAX_DIST
    tiles = [
        lanes(_dist_bias(rel_bias, d0, d0 >= 0)),
        lanes(_dist_bias(rel_bias, d1, d1 >= 0)),
        lanes(_dist_bias(rel_bias, dw, dw < WINDOW)),
        lanes(_dist_bias(rel_bias, far, far >= 0)),
        lanes(_dist_bias(rel_bias, far, far < 0)),
    ]
    diag_bias = jnp.stack(tiles, axis=1)
    ovt = jnp.concatenate([_cmp_sel_overlap(nc - 1, ns), jnp.zeros((1, ns), F32)], axis=0).T
    return cmp_bias, diag_bias, ovt


def _attend_rows(tiles, q4, s_scr):
    seqs = range(len(q4))
    m = [None for _ in seqs]
    for idx in range(len(tiles[0])):
        for b in seqs:
            k, _, add, transposed = tiles[b][idx]
            kb = k.astype(BF16)
            s = (jnp.dot(q4[b], kb, preferred_element_type=F32) if transposed
                 else lax.dot_general(q4[b], kb, NT, preferred_element_type=F32)) + add
            s_scr[b, idx] = s
            mt = jnp.max(s, axis=1, keepdims=True)
            m[b] = mt if m[b] is None else jnp.maximum(m[b], mt)
    l = [None for _ in seqs]
    acc = [None for _ in seqs]
    for idx in range(len(tiles[0])):
        for b in seqs:
            _, v, _, transposed = tiles[b][idx]
            e = jnp.exp(s_scr[b, idx] - m[b])
            lt = jnp.sum(e, axis=1, keepdims=True)
            eb, vb = e.astype(BF16), v.astype(BF16)
            pv = (lax.dot_general(eb, vb, NT, preferred_element_type=F32) if transposed
                  else jnp.dot(eb, vb, preferred_element_type=F32))
            l[b] = lt if l[b] is None else l[b] + lt
            acc[b] = pv if acc[b] is None else acc[b] + pv
    return [acc[b] / jnp.where(l[b] > 0, l[b], 1.0) for b in seqs]


def _nsa_sample_compress_kernel(pt_ref, *refs, n_pages, page, nseq):
    del pt_ref
    pages = [refs[b * n_pages:(b + 1) * n_pages] for b in range(nseq)]
    w1_ref, w2_ref, b_ref, kc_ref, vc_ref, xt_scr = refs[nseq * n_pages:]
    cpt = xt_scr.shape[1]

    spp = page // CMP_STRIDE
    n_str = n_pages * spp
    n_all = nseq * n_str
    r = lax.broadcasted_iota(jnp.int32, (page, page), 0)
    t_in = lax.broadcasted_iota(jnp.int32, (page, page), 1)
    assert spp & (spp - 1) == 0
    stride_of_r = jnp.bitwise_and(r, spp - 1)
    offset_of_r = lax.shift_right_logical(r, int(math.log2(spp)))
    perm = (t_in == stride_of_r * CMP_STRIDE + offset_of_r).astype(BF16)
    for z in range(2):
        for b in range(nseq):
            for k, pg in enumerate(pages[b]):
                for c in range(cpt):
                    slab = pg[0, 0, z, pl.ds(c * LANES, LANES), :].astype(BF16)
                    rows_by_offset = lax.dot_general(perm, slab, NT, preferred_element_type=F32)
                    for i in range(CMP_STRIDE):
                        xt_scr[z, c, pl.ds(i * n_all + b * n_str + k * spp, spp), :] = \
                            rows_by_offset[i * spp:(i + 1) * spp]

    def cached(z):
        return lambda i: jnp.concatenate([xt_scr[z, c, pl.ds(i * n_all, n_all), :] for c in range(cpt)], axis=1)

    res = _compress([cached(0), cached(1)], w1_ref, w2_ref, b_ref)
    for z, dst in enumerate((kc_ref, vc_ref)):
        for b in range(nseq):
            dst[b] = res[z][b * n_str:(b + 1) * n_str].astype(BF16)


def _nsa_sample_attend_kernel(pt_ref, *refs, n_pages, page, tnew, ns, nseq):
    del pt_ref
    pages = [refs[b * n_pages:(b + 1) * n_pages] for b in range(nseq)]
    (ksn_ref, vsn_ref, kwn_ref, vwn_ref, win_ref, q4_ref, gt_ref, kc_ref, vc_ref, cb_ref, sb_ref, wb_ref,
     ov_ref, hs_ref, ex_ref, o_ref, s_scr, w_scr) = refs[nseq * n_pages:]
    g, dh = NSA_KV_GROUPS, NSA_HEAD_DIM
    gd = g * dh
    rows = q4_ref.shape[1]
    rpg = rows // g
    seqs = range(nseq)
    q4 = [q4_ref[b] for b in seqs]
    pad = jnp.zeros((page - tnew, gd), F32)

    def new_rows(ref, b):
        return jnp.concatenate([ref[pl.ds(b * tnew, tnew), :], pad], axis=0)

    w_tiles = win_ref.shape[4] // page
    tiles = []
    for b in seqs:
        tl = [(win_ref[0, b, 0, :, pl.ds(k * page, page)], win_ref[0, b, 1, :, pl.ds(k * page, page)], wb_ref[k], True)
              for k in range(w_tiles)]
        tl.append((new_rows(kwn_ref, b), new_rows(vwn_ref, b), wb_ref[w_tiles], False))
        tiles.append(tl)
    owin = _attend_rows(tiles, q4, w_scr)

    cb = cb_ref[...]
    s = [lax.dot_general(q4[b], kc_ref[b], NT, preferred_element_type=F32) + cb for b in seqs]
    e = [jnp.where(cb > VIS, jnp.exp(s[b] - jnp.max(s[b], axis=1, keepdims=True)), 0.0) for b in seqs]
    den = [jnp.sum(e[b], axis=1, keepdims=True) for b in seqs]
    pc = [e[b] / jnp.where(den[b] > 0, den[b], 1.0) for b in seqs]
    oc = [jnp.dot(pc[b].astype(BF16), vc_ref[b], preferred_element_type=F32) for b in seqs]

    imp = [_dot_f32(hs_ref[...], _dot_f32(pc[b], ov_ref[...])) for b in seqs]
    nsp = imp[0].shape[1]
    blk = lax.broadcasted_iota(jnp.int32, (rows, nsp), 1)
    cur = ns - 1
    forced = (blk == 0) | (blk == cur) | (blk == cur - 1)
    nsb = -(-ns // 8) * 8
    blk_t = blk.T[0:nsb].astype(F32)
    sel = []
    for b in seqs:
        score = jnp.where(blk < ns, imp[b] + FORCE_BONUS * forced.astype(F32), LOWEST)
        sel_t = _top_k_mask(score.T[0:nsb], blk_t, 0, min(SEL_TOP_N, ns))
        sel.append(jnp.concatenate([sel_t, jnp.zeros((nsp - nsb, rows), F32)], axis=0).T.astype(BF16))

    def hide(b, k):
        return jnp.where(jnp.dot(sel[b], ex_ref[k], preferred_element_type=F32) > 0.5, 0.0, NEG)

    tiles = []
    for b in seqs:
        tl = [(pg[0, 0, 0], pg[0, 0, 1], sb_ref[k] + hide(b, k), True) for k, pg in enumerate(pages[b])]
        tl.append((new_rows(ksn_ref, b), new_rows(vsn_ref, b), sb_ref[n_pages] + hide(b, n_pages), False))
        tiles.append(tl)
    osel = _attend_rows(tiles, q4, s_scr)

    for b in seqs:
        gt = jax.nn.sigmoid(gt_ref[b])
        o = gt[:, 0:1] * oc[b] + gt[:, 1:2] * osel[b] + gt[:, 2:3] * owin[b]
        for gi in range(g):
            o_ref[b, gi] = o[gi * rpg:(gi + 1) * rpg, gi * dh:(gi + 1) * dh].astype(BF16)


def nsa_sample(p, row0, cache_t, layer, win_t, page_table, q4, gates, tables, cweights, *, tnew):
    n, n_pages = page_table.shape
    g, dh = NSA_KV_GROUPS, NSA_HEAD_DIM
    gd = g * dh
    page = cache_t.shape[4]
    rows = q4.shape[1]
    cb, sb, wb, ov, hs, ex = tables
    w1, w2, hid0 = cweights
    ns = -(-(n_pages * page + tnew) // SEL_BLOCK)
    assert row0 % tnew == 0 and page % SEL_BLOCK == 0 and page == LANES and tnew <= page
    assert (n_pages * page + tnew) // CMP_STRIDE == n_pages * page // CMP_STRIDE
    blk0 = row0 // tnew
    q_blocks = (rows // tnew * dh) // gd
    assert gd % LANES == 0
    wbuf = win_t.shape[4]
    assert wbuf % page == 0

    nseq = 2 if n % 2 == 0 and row0 % (2 * tnew) == 0 else 1
    n_str = n_pages * page // CMP_STRIDE
    half = NSA_CACHED // 2

    def page_specs(tblk):
        return [pl.BlockSpec((1, 1, half, gd, page),
                             lambda i, pt, b=b, k=k: (layer, pt[(i * nseq + b) * n_pages + k], tblk, 0, 0))
                for b in range(nseq) for k in range(n_pages)]

    def new_spec(z):
        return pl.BlockSpec((nseq * tnew, gd), lambda i, pt, z=z: (blk0 // nseq + i, q_blocks + z))

    def const(shape):
        return pl.BlockSpec(shape, lambda i, pt: (0,) * len(shape))

    def per_seq(shape):
        return pl.BlockSpec((nseq,) + shape, lambda i, pt: (i,) + (0,) * len(shape))

    pages = [cache_t] * (nseq * n_pages)
    pt = page_table.reshape(-1)
    kc, vc = pl.pallas_call(
        functools.partial(_nsa_sample_compress_kernel, n_pages=n_pages, page=page, nseq=nseq),
        grid_spec=pltpu.PrefetchScalarGridSpec(
            num_scalar_prefetch=1,
            grid=(n // nseq,),
            in_specs=page_specs(0) + [const(w1.shape), const(w2.shape), const(hid0.shape)],
            out_specs=[per_seq((n_str, gd))] * 2,
            scratch_shapes=[pltpu.VMEM((2, gd // LANES, nseq * n_pages * page, LANES), F32)],
        ),
        out_shape=[jax.ShapeDtypeStruct((n, n_str, gd), BF16)] * 2,
        compiler_params=_params(("arbitrary",)),
        name="nsa_sample_compress",
    )(pt, *pages, w1, w2, hid0)
    return pl.pallas_call(
        functools.partial(_nsa_sample_attend_kernel, n_pages=n_pages, page=page, tnew=tnew, ns=ns, nseq=nseq),
        grid_spec=pltpu.PrefetchScalarGridSpec(
            num_scalar_prefetch=1,
            grid=(n // nseq,),
            in_specs=page_specs(1) + [
                new_spec(2), new_spec(3), new_spec(4), new_spec(5),
                pl.BlockSpec((1, nseq, 2, gd, wbuf), lambda i, pt: (layer, i, 0, 0, 0)),
                per_seq((rows, gd)), per_seq((rows, 3)), per_seq((n_str, gd)), per_seq((n_str, gd)),
                const(cb.shape), const(sb.shape), const(wb.shape), const(ov.shape), const(hs.shape),
                const(ex.shape),
            ],
            out_specs=per_seq((g, rows // g, dh)),
            scratch_shapes=[pltpu.VMEM((nseq, n_pages + 1, rows, page), F32),
                            pltpu.VMEM((nseq, wbuf // page + 1, rows, page), F32)],
        ),
        out_shape=jax.ShapeDtypeStruct((n, g, rows // g, dh), BF16),
        compiler_params=_params(("arbitrary",)),
        name="nsa_sample_attend",
    )(pt, *pages, p, p, p, p, win_t, q4, gates, kc, vc, cb, sb, wb, ov, hs, ex)


def _nsa_sample_tables(rel_bias, past, tnew, wbuf, page, hpg):
    g = NSA_KV_GROUPS
    h = g * hpg
    qpos = past + jnp.arange(tnew)
    n_str = (past + tnew) // CMP_STRIDE
    ns = -(-(past + tnew) // SEL_BLOCK)
    nsp = LANES
    assert ns <= nsp

    def rows(b):
        return b.reshape(h * tnew, b.shape[2])

    def tiled(b, n_tiles):
        return b.reshape(h * tnew, n_tiles, page).transpose(1, 0, 2)

    cmp_end = jnp.arange(n_str) * CMP_STRIDE + CMP_BLOCK - 1
    dc = qpos[:, None] - cmp_end[None, :]
    cb = rows(_dist_bias(rel_bias, dc, (dc >= 0) & (jnp.arange(n_str) < n_str - 1)[None, :]))
    n_tiles = past // page + 1
    kpos = jnp.arange(n_tiles * page)
    ds = qpos[:, None] - kpos[None, :]
    sb = tiled(rows(_dist_bias(rel_bias, ds, (ds >= 0) & (kpos < past + tnew)[None, :])), n_tiles)
    w_tiles = wbuf // page + 1
    r = jnp.arange(w_tiles * page)
    dw = qpos[:, None] - (past - wbuf + r)[None, :]
    wb = tiled(rows(_dist_bias(rel_bias, dw, (dw >= 0) & (dw < WINDOW) & (r < wbuf + tnew)[None, :])), w_tiles)
    ov = jnp.zeros((n_str, nsp), F32).at[:n_str - 1, :ns].set(_cmp_sel_overlap(n_str - 1, ns))
    row = jnp.arange(h * tnew)
    same = (row[:, None] // (hpg * tnew) == row[None, :] // (hpg * tnew)) & \
           (row[:, None] % tnew == row[None, :] % tnew)
    expand = (kpos[None, :] // SEL_BLOCK == jnp.arange(nsp)[:, None]).astype(BF16)
    expand = expand.reshape(nsp, n_tiles, page).transpose(1, 0, 2)
    return cb, sb, wb, ov, same.astype(F32), expand


def _softplus(x):
    return jnp.maximum(x, 0.0) + jnp.log1p(jnp.exp(-jnp.abs(x)))


def _transpose_rows(x):
    r = x.shape[0]
    if r >= GDN_CHUNK:
        return x.T
    return jnp.concatenate([x, jnp.zeros((LANES - r, x.shape[1]), x.dtype)], axis=0).T[:, 0:r]


def _unit_lower_inverse(a):
    c = a[0].shape[0]
    eye = (lax.broadcasted_iota(jnp.int32, (c, c), 0) == lax.broadcasted_iota(jnp.int32, (c, c), 1)).astype(F32)
    inv = [eye - x for x in a]
    power = a
    n = 2
    while n < c:
        power = [_dot(x, x) for x in power]
        inv = [y + _dot(y, x) for x, y in zip(power, inv)]
        n *= 2
    resid = [(eye - y) - _dot_split(x, y) for x, y in zip(a, inv)]
    return [y + _dot(y, r) for y, r in zip(inv, resid)]


def _gdn_kernel(qkv_ref, z_ref, ab_ref, hist_ref, s0_ref, cw_ref, nar_ref, dtr_ref, nac_ref, dtc_ref,
                nw_ref, *rest, c, tv, qk_heads, v_heads, n_prev):
    prev_refs = rest[:n_prev]
    o_ref, sfin_ref, s_scr, carry_scr = rest[n_prev:]
    hd = GDN_HEAD
    halo = carry_scr.shape[0] - tv
    step = pl.program_id(1)

    @pl.when(step == 0)
    def _():
        s_scr[...] = s0_ref[0, 0]
        carry_scr[pl.ds(0, halo), :] = hist_ref[0]

    carry_scr[pl.ds(halo, tv), :] = qkv_ref[...]
    off = halo - (GDN_CONV - 1)
    conv = carry_scr[pl.ds(off, tv), :] * cw_ref[0:1]
    for i in range(1, GDN_CONV):
        conv = conv + carry_scr[pl.ds(off + i, tv), :] * cw_ref[i:i + 1]
    carry_scr[pl.ds(0, halo), :] = carry_scr[pl.ds(tv, halo), :]
    act = conv * jax.nn.sigmoid(conv)
    if tv < c:
        act = jnp.concatenate([act, jnp.zeros((c - tv, act.shape[1]), F32)], axis=0)
        ab = jnp.concatenate([ab_ref[...], jnp.zeros((c - tv, ab_ref.shape[1]), F32)], axis=0)
    else:
        ab = ab_ref[...]

    row = lax.broadcasted_iota(jnp.int32, (c, c), 0)
    col = lax.broadcasted_iota(jnp.int32, (c, c), 1)
    incl = row >= col
    strict = row > col
    lower = incl.astype(F32)
    upper = (row <= col).astype(F32)

    live_c = lax.broadcasted_iota(jnp.int32, ab.shape, 0) < tv
    g_col = jnp.where(live_c, nar_ref[...] * _softplus(ab + dtr_ref[...]), 0.0)
    beta = jnp.where(live_c, jax.nn.sigmoid(ab), 0.0)
    abt = _transpose_rows(ab)[0:v_heads]
    live_r = lax.broadcasted_iota(jnp.int32, abt.shape, 1) < tv
    g_row = jnp.where(live_r, nac_ref[...] * _softplus(abt + dtc_ref[...]), 0.0)
    gam_col = _dot_f32(lower, g_col)
    gam_row = _dot_f32(g_row, upper)

    rep = v_heads // qk_heads
    hvs = range(v_heads)
    qn, kn, kk, qk = [], [], [], []
    for hq in range(qk_heads):
        qh = act[:, hq * hd:(hq + 1) * hd]
        kh = act[:, (qk_heads + hq) * hd:(qk_heads + hq + 1) * hd]
        qn.append(qh * lax.rsqrt(jnp.sum(qh * qh, axis=-1, keepdims=True) + 1e-6) * (hd ** -0.5))
        kn.append(kh * lax.rsqrt(jnp.sum(kh * kh, axis=-1, keepdims=True) + 1e-6))
    for hq in range(qk_heads):
        kk.append(_dot_nt(kn[hq], kn[hq]))
        qk.append(_dot_nt(qn[hq], kn[hq]))
    gc = [gam_col[:, hv:hv + 1] for hv in hvs]
    bc = [beta[:, v_heads + hv:v_heads + hv + 1] for hv in hvs]
    decay = [jnp.where(incl, jnp.exp(jnp.where(incl, gc[hv] - gam_row[hv:hv + 1, :], 0.0)), 0.0) for hv in hvs]
    a_mat = [jnp.where(strict, bc[hv] * kk[hv // rep] * decay[hv], 0.0) for hv in hvs]
    tinv = _unit_lower_inverse(a_mat)
    eg = [jnp.exp(gc[hv]) for hv in hvs]
    g_last = [gc[hv][c - 1:c, :] for hv in hvs]
    rhs = [jnp.concatenate([act[:, (2 * qk_heads + hv) * hd:(2 * qk_heads + hv + 1) * hd] * bc[hv],
                            kn[hv // rep] * (bc[hv] * eg[hv])], axis=1) for hv in hvs]
    vk = [_dot(tinv[hv], rhs[hv]) for hv in hvs]
    s_old = [s_scr[hv] for hv in hvs]
    ks = [_dot(jnp.concatenate([vk[hv][:, hd:], qn[hv // rep] * eg[hv]], axis=0), s_old[hv]) for hv in hvs]
    u = [vk[hv][:, :hd] - ks[hv][:c] for hv in hvs]
    k_dec = [_transpose_rows(kn[hv // rep] * jnp.exp(g_last[hv] - gc[hv])) for hv in hvs]
    upd = [_dot(jnp.concatenate([qk[hv // rep] * decay[hv], k_dec[hv]], axis=0), u[hv]) for hv in hvs]
    for hv in hvs:
        s_scr[hv] = s_old[hv] * jnp.exp(g_last[hv]) + upd[hv][c:]
        o = (ks[hv][c:] + upd[hv][:c])[0:tv]
        zh = z_ref[:, hv * hd:(hv + 1) * hd]
        o = _rms(o, nw_ref[...]) * (zh * jax.nn.sigmoid(zh))
        o_ref[:, hv * hd:(hv + 1) * hd] = o.astype(BF16)

    @pl.when(step == pl.num_programs(1) - 1)
    def _():
        for l, prev in enumerate(prev_refs):
            sfin_ref[l, 0] = prev[0]
        sfin_ref[n_prev, 0] = s_scr[...]


def gdn(p, row0, n, t, hist, s0, layer, conv_w, a_log, dt_bias, norm_w, *, qk_heads, v_heads, prev=()):
    hd = GDN_HEAD
    ch = (2 * qk_heads + v_heads) * hd
    vw = v_heads * hd
    tv = min(GDN_CHUNK, t)
    c = max(tv, 16)
    assert t % tv == 0 and row0 % tv == 0 and ch % vw == 0
    nchunk = t // tv
    blk0 = row0 // tv
    halo = 8
    histp = jnp.concatenate([jnp.zeros((n, halo - (GDN_CONV - 1), ch), F32), hist.astype(F32)], axis=1)
    cw = jnp.concatenate([conv_w.astype(F32), jnp.zeros((halo - GDN_CONV, ch), F32)], axis=0)
    neg_a = -jnp.exp(a_log.astype(F32))
    dt = dt_bias.astype(F32)
    pad = jnp.zeros((LANES - v_heads,), F32)
    nar = jnp.concatenate([neg_a, pad]).reshape(1, LANES)
    dtr = jnp.concatenate([dt, pad]).reshape(1, LANES)
    n_prev = len(prev)
    state_spec = pl.BlockSpec((1, v_heads, hd, hd), lambda b, s: (b, 0, 0, 0))
    o, sfin = pl.pallas_call(
        functools.partial(_gdn_kernel, c=c, tv=tv, qk_heads=qk_heads, v_heads=v_heads, n_prev=n_prev),
        grid=(n, nchunk),
        in_specs=[
            pl.BlockSpec((tv, ch), lambda b, s: (blk0 + b * nchunk + s, 0)),
            pl.BlockSpec((tv, vw), lambda b, s: (blk0 + b * nchunk + s, ch // vw)),
            pl.BlockSpec((tv, LANES), lambda b, s: (blk0 + b * nchunk + s, (ch + vw) // LANES)),
            pl.BlockSpec((1, halo, ch), lambda b, s: (b, 0, 0)),
            pl.BlockSpec((1, 1, v_heads, hd, hd), lambda b, s: (layer, b, 0, 0, 0)),
            pl.BlockSpec((halo, ch), lambda b, s: (0, 0)),
            pl.BlockSpec((1, LANES), lambda b, s: (0, 0)),
            pl.BlockSpec((1, LANES), lambda b, s: (0, 0)),
            pl.BlockSpec((v_heads, 1), lambda b, s: (0, 0)),
            pl.BlockSpec((v_heads, 1), lambda b, s: (0, 0)),
            pl.BlockSpec((1, hd), lambda b, s: (0, 0)),
        ] + [state_spec] * n_prev,
        out_specs=[
            pl.BlockSpec((tv, vw), lambda b, s: (b * nchunk + s, 0)),
            pl.BlockSpec((n_prev + 1, 1, v_heads, hd, hd), lambda b, s: (0, b, 0, 0, 0)),
        ],
        out_shape=[jax.ShapeDtypeStruct((n * t, vw), BF16),
                   jax.ShapeDtypeStruct((n_prev + 1, n, v_heads, hd, hd), F32)],
        scratch_shapes=[pltpu.VMEM((v_heads, hd, hd), F32), pltpu.VMEM((halo + tv, ch), F32)],
        compiler_params=_params(("parallel", "arbitrary")),
        name="gdn",
    )(p, p, p, histp, s0.astype(F32), cw, nar, dtr, neg_a.reshape(v_heads, 1), dt.reshape(v_heads, 1),
      norm_w.astype(F32).reshape(1, hd), *prev)
    return o, sfin


def _pow2_divisor(m, cap):
    tile = cap
    while m % tile:
        tile //= 2
    return tile


def _window(p, r0, r1, c0, c1):
    return lax.slice(p, (r0, c0), (r1, c1))


def _pad_cols(w, mult):
    n = w.shape[-1]
    return jnp.pad(w, ((0, 0), (0, -n % mult)))


def kernel(x_prompt, x_sample, cache_kv, cache_win, state_ssm, state_conv, page_table, rel_bias, norm_mix, norm_ffn, norm_final, nsa_w_in, nsa_w_out, nsa_cmp_w1, nsa_cmp_w2, nsa_cmp_pe, gdn_w_in, gdn_conv_w, gdn_a_log, gdn_dt_bias, gdn_norm_w, gdn_w_out, ffn_w1, ffn_w2):
    nb, t, d = x_prompt.shape
    ns_, tnew, _ = x_sample.shape
    depth = norm_mix.shape[0]
    g, dh = NSA_KV_GROUPS, NSA_HEAD_DIM
    gd = g * dh
    heads = nsa_w_out.shape[1] // dh
    hpg = heads // g
    qw = heads * dh
    mp = nb * t
    past = page_table.shape[1] * cache_kv.shape[2]
    wbuf = cache_win.shape[2]
    page = cache_kv.shape[2]
    v_heads = gdn_a_log.shape[1]
    qk_heads = (gdn_conv_w.shape[2] // GDN_HEAD - v_heads) // 2
    conv_ch = gdn_conv_w.shape[2]
    m_all = nb * t + ns_ * tnew
    tn_nsa = -(-nsa_w_in.shape[2] // LANES) * LANES
    tn_gdn = 10 * LANES
    tm_nsa = _pow2_divisor(m_all, 1024)
    tm_gdn = _pow2_divisor(m_all, 1024)
    tm_ffn = _pow2_divisor(math.gcd(mp, ns_ * tnew), 512)

    x = jnp.concatenate([x_prompt.reshape(mp, d), x_sample.reshape(ns_ * tnew, d)], axis=0).astype(F32)
    tq = LANES
    p_tables = _nsa_prompt_t_tables(rel_bias, t, tq, hpg)
    s_tables = _nsa_sample_tables(rel_bias, past, tnew, wbuf, page, hpg)
    cache_t = cache_kv.transpose(0, 1, 3, 4, 5, 2).reshape(cache_kv.shape[0], cache_kv.shape[1], NSA_CACHED, gd, page)
    win_t = cache_win.transpose(0, 1, 3, 4, 5, 2).reshape(cache_win.shape[0], ns_, 2, gd, wbuf)

    kv_p, kv_s, win_p, win_s, ssm_p, ssm_s, conv_p, conv_s = [], [], [], [], [], [], [], []
    for i in range(depth):
        li = i // 2
        if i % 2 == 0:
            w_in = _pad_cols(nsa_w_in[li], tn_nsa).astype(BF16)
            p = norm_matmul(x, norm_mix[i], w_in, tm=tm_nsa, tn=tn_nsa)
            kv_end = qw + NSA_N_KV * gd
            ps = _window(p, mp, m_all, 0, kv_end + 3 * heads).reshape(ns_, tnew, -1)
            cweights = _compress_weights(nsa_cmp_w1[li], nsa_cmp_w2[li], nsa_cmp_pe[li])
            kc, vct = nsa_compress(p, nb, t, qw, *cweights)
            kv4 = _window(p, 0, mp, qw + 2 * gd, kv_end).reshape(nb, t, 2, 2, g, dh)
            kt = kv4[:, :, :, 0].transpose(2, 0, 3, 1, 4).astype(BF16)
            vt = kv4[:, :, :, 1].transpose(2, 0, 3, 4, 1).astype(BF16)
            aug = jnp.zeros((2, nb, g, V_AUG, t), BF16).at[:, :, :, 0].set(1.0)
            vt = jnp.concatenate([vt, aug], axis=3).reshape(2, nb, g, dh + V_AUG, t // tq, tq)
            vt = vt.transpose(0, 1, 2, 4, 3, 5)
            gates_p = _window(p, 0, mp, kv_end, kv_end + 3 * heads).reshape(mp, g, 3 * hpg).transpose(1, 2, 0)
            o_p = nsa_prompt_t(p, gates_p, kc, vct, kt, vt, *p_tables, nb, t, tq=tq, sub=4)
            q_s = (ps[:, :, :qw] * dh ** -0.5).reshape(ns_, tnew, g, hpg, dh)
            q4 = jnp.einsum("ntghd,gk->nghtkd", q_s, jnp.eye(g, dtype=F32))
            q4 = q4.reshape(ns_, heads * tnew, gd).astype(BF16)
            gates_s = ps[:, :, kv_end:kv_end + 3 * heads].reshape(ns_, tnew, heads, 3)
            gates_s = gates_s.transpose(0, 2, 1, 3).reshape(ns_, heads * tnew, 3)
            o_s = nsa_sample(p, mp, cache_t, li, win_t, page_table, q4, gates_s, s_tables, cweights, tnew=tnew)
            o_s = o_s.reshape(ns_, g, hpg, tnew, dh).transpose(0, 3, 1, 2, 4).reshape(ns_ * tnew, qw)
            w_out = nsa_w_out[li].astype(BF16)
            kv_p.append(_window(p, 0, mp, qw, qw + NSA_CACHED * gd).reshape(nb, t, NSA_CACHED, g, dh))
            kv_s.append(ps[:, :, qw:qw + NSA_CACHED * gd].reshape(ns_, tnew, NSA_CACHED, g, dh))
            wkeep = min(WINDOW, t)
            win_p.append(jnp.stack([_window(p, b * t + t - wkeep, (b + 1) * t, qw + NSA_CACHED * gd, kv_end)
                                    for b in range(nb)]).reshape(nb, wkeep, 2, g, dh))
            win_s.append(ps[:, :, qw + NSA_CACHED * gd:kv_end].reshape(ns_, tnew, 2, g, dh))
        else:
            w_in = _pad_cols(gdn_w_in[li], tn_gdn).astype(BF16)
            p = norm_matmul(x, norm_mix[i], w_in, tm=tm_gdn, tn=tn_gdn)
            args = (gdn_conv_w[li], gdn_a_log[li], gdn_dt_bias[li], gdn_norm_w[li])
            h0 = jnp.zeros((nb, GDN_CONV - 1, conv_ch), F32)
            s0 = jnp.zeros((1, nb, v_heads, GDN_HEAD, GDN_HEAD), F32)
            last = i + 2 >= depth
            o_p, sp = gdn(p, 0, nb, t, h0, s0, 0, *args, qk_heads=qk_heads, v_heads=v_heads,
                          prev=tuple(ssm_p) if last else ())
            o_s, ss = gdn(p, mp, ns_, tnew, state_conv[li], state_ssm, li, *args,
                          qk_heads=qk_heads, v_heads=v_heads, prev=tuple(ssm_s) if last else ())
            sp, ss = (sp, ss) if last else (sp[0], ss[0])
            w_out = gdn_w_out[li].astype(BF16)
            ssm_p.append(sp.astype(state_ssm.dtype))
            ssm_s.append(ss.astype(state_ssm.dtype))
            keep = GDN_CONV - 1
            assert t >= keep and tnew >= keep
            conv_p.append(jnp.stack([_window(p, (b + 1) * t - keep, (b + 1) * t, 0, conv_ch) for b in range(nb)]))
            conv_s.append(_window(p, mp, m_all, 0, conv_ch).reshape(ns_, tnew, conv_ch)[:, tnew - keep:])
        x = mix_ffn(x, o_p, o_s, w_out, norm_ffn[i], ffn_w1[i].astype(BF16), ffn_w2[i].astype(BF16), norm_final,
                    final_norm=(i == depth - 1), tm=tm_ffn, th=1024)
    y_prompt = x[:mp].reshape(nb, t, d)
    y_sample = x[mp:].reshape(ns_, tnew, d)
    new_win = jnp.stack(win_s).astype(cache_win.dtype)
    win_s_all = jnp.concatenate([cache_win[:new_win.shape[0]], new_win], axis=2)[:, :, -wbuf:]
    return (y_prompt, y_sample, jnp.stack(kv_p), jnp.stack(kv_s), jnp.stack(win_p), win_s_all,
            ssm_p[-1], ssm_s[-1], jnp.stack(conv_p), jnp.stack(conv_s))
```

```python
import functools
import math

import jax
import jax.numpy as jnp
from jax import lax
from jax.experimental import pallas as pl
from jax.experimental.pallas import tpu as pltpu

F32 = jnp.float32
BF16 = jnp.bfloat16
HIGHEST = lax.Precision.HIGHEST

LANES = 128
VMEM_LIMIT = 56 * 1024 * 1024

NSA_HEAD_DIM = 64
NSA_KV_GROUPS = 4
NSA_N_KV = 6
NSA_CACHED = 4
CMP_BLOCK = 32
CMP_STRIDE = 16
SEL_BLOCK = 64
SEL_TOP_N = 8
WINDOW = 512
FORCE_BONUS = 1e4
REL_BUCKETS = 32
REL_MAX_DIST = 128
GDN_HEAD = 128
GDN_CONV = 4
GDN_CHUNK = 64
RMS_EPS = 1e-6
NEG = -1e30
VIS = -1e29
LOWEST = -3e38

NT = (((1,), (1,)), ((), ()))


def _params(sem):
    return pltpu.CompilerParams(dimension_semantics=sem, vmem_limit_bytes=VMEM_LIMIT)


def _dot(a, b):
    return jnp.dot(a.astype(BF16), b.astype(BF16), preferred_element_type=F32)


def _dot_nt(a, b):
    return lax.dot_general(a.astype(BF16), b.astype(BF16), NT, preferred_element_type=F32)


def _dot_split(a, b):
    ah, bh = a.astype(BF16), b.astype(BF16)
    al = (a - ah.astype(F32)).astype(BF16)
    bl = (b - bh.astype(F32)).astype(BF16)
    hh = jnp.dot(ah, bh, preferred_element_type=F32)
    return hh + (jnp.dot(ah, bl, preferred_element_type=F32) + jnp.dot(al, bh, preferred_element_type=F32))


def _dot_f32(a, b):
    return jnp.dot(a, b, precision=HIGHEST, preferred_element_type=F32)


def _rms(x, gain):
    ms = jnp.mean(x * x, axis=-1, keepdims=True)
    return x * lax.rsqrt(ms + RMS_EPS) * gain


def _norm_matmul_kernel(x_ref, g_ref, w_ref, o_ref, xn_ref):
    @pl.when(pl.program_id(1) == 0)
    def _():
        xn_ref[...] = _rms(x_ref[...], g_ref[...]).astype(BF16)

    o_ref[...] = jnp.dot(xn_ref[...], w_ref[...], preferred_element_type=F32)


def norm_matmul(x, gains, gi, ws, wi, *, tm, tn):
    m, d = x.shape
    n = ws.shape[2]
    assert m % tm == 0 and n % tn == 0
    return pl.pallas_call(
        _norm_matmul_kernel,
        grid=(m // tm, n // tn),
        in_specs=[
            pl.BlockSpec((tm, d), lambda i, j: (i, 0)),
            pl.BlockSpec((None, 1, d), lambda i, j: (gi, 0, 0)),
            pl.BlockSpec((None, d, tn), lambda i, j: (wi, 0, j)),
        ],
        out_specs=pl.BlockSpec((tm, tn), lambda i, j: (i, j)),
        out_shape=jax.ShapeDtypeStruct((m, n), F32),
        scratch_shapes=[pltpu.VMEM((tm, d), BF16)],
        compiler_params=_params(("parallel", "arbitrary")),
        name="norm_matmul",
    )(x, gains, ws)


def _mix_ffn_kernel(x_ref, oa_ref, ob_ref, wo_ref, g_ref, w1_ref, w2_ref, gf_ref, y_ref,
                    x1_ref, xn_ref, acc_ref, *, final_norm, tiles_a):
    j = pl.program_id(1)

    @pl.when(j == 0)
    def _():
        o = jnp.where(pl.program_id(0) < tiles_a, oa_ref[...], ob_ref[...])
        x1 = x_ref[...] + jnp.dot(o, wo_ref[...], preferred_element_type=F32)
        x1_ref[...] = x1
        xn_ref[...] = _rms(x1, g_ref[...]).astype(BF16)
        acc_ref[...] = jnp.zeros_like(acc_ref)

    hid = jnp.maximum(jnp.dot(xn_ref[...], w1_ref[...], preferred_element_type=F32), 0.0)
    acc_ref[...] += jnp.dot((hid * hid).astype(BF16), w2_ref[...], preferred_element_type=F32)

    @pl.when(j == pl.num_programs(1) - 1)
    def _():
        y = x1_ref[...] + acc_ref[...]
        if final_norm:
            y = _rms(y, gf_ref[...])
        y_ref[...] = y


def mix_ffn(x, o_a, o_b, w_outs, wo_i, gains, w1s, w2s, layer, gain_final, *, final_norm, tm, th):
    m, d = x.shape
    ko = o_a.shape[1]
    hdim = w1s.shape[2]
    assert m % tm == 0 and hdim % th == 0 and o_a.shape[0] % tm == 0 and o_a.shape[0] + o_b.shape[0] == m
    tiles_a = o_a.shape[0] // tm
    return pl.pallas_call(
        functools.partial(_mix_ffn_kernel, final_norm=final_norm, tiles_a=tiles_a),
        grid=(m // tm, hdim // th),
        in_specs=[
            pl.BlockSpec((tm, d), lambda i, j: (i, 0)),
            pl.BlockSpec((tm, ko), lambda i, j: (jnp.minimum(i, tiles_a - 1), 0)),
            pl.BlockSpec((tm, ko), lambda i, j: (jnp.maximum(i - tiles_a, 0), 0)),
            pl.BlockSpec((None, ko, d), lambda i, j: (wo_i, 0, 0)),
            pl.BlockSpec((None, 1, d), lambda i, j: (layer, 0, 0)),
            pl.BlockSpec((None, d, th), lambda i, j: (layer, 0, j)),
            pl.BlockSpec((None, th, d), lambda i, j: (layer, j, 0)),
            pl.BlockSpec((1, d), lambda i, j: (0, 0)),
        ],
        out_specs=pl.BlockSpec((tm, d), lambda i, j: (i, 0)),
        out_shape=jax.ShapeDtypeStruct((m, d), F32),
        scratch_shapes=[pltpu.VMEM((tm, d), F32), pltpu.VMEM((tm, d), BF16), pltpu.VMEM((tm, d), F32)],
        compiler_params=_params(("parallel", "arbitrary")),
        name="mix_ffn",
    )(x, o_a, o_b, w_outs, gains, w1s, w2s, gain_final.reshape(1, d))


def _rel_bucket(dist):
    n = jnp.maximum(dist, 0)
    max_exact = REL_BUCKETS // 2
    nf = jnp.maximum(n, 1).astype(F32)
    large = max_exact + (jnp.log(nf / max_exact) / math.log(REL_MAX_DIST / max_exact)
                         * (REL_BUCKETS - max_exact)).astype(jnp.int32)
    return jnp.where(n < max_exact, n, jnp.minimum(large, REL_BUCKETS - 1))


def _dist_bias(rel_bias, dist, visible):
    tbl = rel_bias.astype(F32)
    onehot = (_rel_bucket(dist)[..., None] == jnp.arange(REL_BUCKETS)).astype(F32)
    b = jnp.einsum("...k,kh->h...", onehot, tbl, precision=HIGHEST)
    return jnp.where(visible[None], b, NEG)


def _cmp_sel_overlap(nc, ns):
    c0 = jnp.arange(nc) * CMP_STRIDE
    s0 = jnp.arange(ns) * SEL_BLOCK
    ov = jnp.minimum(c0[:, None] + CMP_BLOCK, s0[None, :] + SEL_BLOCK) - jnp.maximum(c0[:, None], s0[None, :])
    return jnp.maximum(ov, 0).astype(F32) / CMP_BLOCK


def _compress_weights(cw1, cw2, cpe):
    g = NSA_KV_GROUPS
    eye = jnp.eye(g, dtype=F32)
    w1 = jnp.einsum("zide,gh->zigdhe", cw1.astype(F32), eye)
    w1 = w1.reshape(2, CMP_BLOCK, g * NSA_HEAD_DIM, g * NSA_HEAD_DIM).astype(BF16)
    w2 = jnp.einsum("zde,gh->zgdhe", cw2.astype(F32), eye)
    w2 = w2.reshape(2, g * NSA_HEAD_DIM, g * NSA_HEAD_DIM).astype(BF16)
    hid0 = jnp.einsum("zid,zide->ze", cpe.astype(F32), cw1.astype(F32), precision=HIGHEST)
    hid0 = jnp.tile(hid0[:, None, :], (1, 1, g))
    return w1, w2, hid0


def _compress(load_rows, w1_ref, w2_ref, b_ref):
    r = CMP_BLOCK // CMP_STRIDE
    assert r == 2
    types = range(len(load_rows))
    acc_a = [None for _ in types]
    acc_b = [None for _ in types]
    for i in range(CMP_STRIDE):
        for z in types:
            xi = load_rows[z](i).astype(BF16)
            da = jnp.dot(xi, w1_ref[z, i], preferred_element_type=F32)
            db = jnp.dot(xi, w1_ref[z, CMP_STRIDE + i], preferred_element_type=F32)
            acc_a[z] = da if acc_a[z] is None else acc_a[z] + da
            acc_b[z] = db if acc_b[z] is None else acc_b[z] + db
    n_str = acc_a[0].shape[0]
    out = []
    for z in types:
        hid = acc_a[z] + pltpu.roll(acc_b[z], n_str - 1, axis=0) + b_ref[z]
        act = hid * jax.nn.sigmoid(hid)
        out.append(jnp.dot(act.astype(BF16), w2_ref[z], preferred_element_type=F32))
    return out


def _top_k_mask(score, index, axis, k):
    sel = jnp.zeros(score.shape, F32)
    for _ in range(k):
        mx = jnp.max(score, axis=axis, keepdims=True)
        cand = jnp.where(score == mx, index, 1e9)
        first = jnp.min(cand, axis=axis, keepdims=True)
        hit = index == first
        sel = jnp.where(hit, 1.0, sel)
        score = jnp.where(hit, LOWEST, score)
    return sel


def _nsa_compress_kernel(*refs, n_str, n_half):
    g, dh = NSA_KV_GROUPS, NSA_HEAD_DIM
    srcs = (refs[:n_half], refs[n_half:2 * n_half])
    w1_ref, w2_ref, b_ref, kc_ref, vc_ref = refs[2 * n_half:]
    def load_rows(z):
        return lambda i: jnp.concatenate([r[pl.ds(i, n_str, stride=CMP_STRIDE), :] for r in srcs[z]], axis=1)

    kc, vc = _compress([load_rows(0), load_rows(1)], w1_ref, w2_ref, b_ref)
    vct = vc.T
    for gi in range(g):
        kc_ref[0, gi] = kc[:, gi * dh:(gi + 1) * dh].astype(BF16)
        vc_ref[0, gi] = vct[gi * dh:(gi + 1) * dh].astype(BF16)


def nsa_compress(p, n, t, qw, w1, w2, hid0):
    gd = NSA_KV_GROUPS * NSA_HEAD_DIM
    n_str = t // CMP_STRIDE
    assert qw % LANES == 0 and gd % LANES == 0
    n_half = gd // LANES
    q_blocks = qw // LANES
    out = [jax.ShapeDtypeStruct((n, NSA_KV_GROUPS, n_str, NSA_HEAD_DIM), BF16),
           jax.ShapeDtypeStruct((n, NSA_KV_GROUPS, NSA_HEAD_DIM, n_str), BF16)]
    return pl.pallas_call(
        functools.partial(_nsa_compress_kernel, n_str=n_str, n_half=n_half),
        grid=(n,),
        in_specs=[pl.BlockSpec((t, LANES), lambda i, c=c: (i, q_blocks + c)) for c in range(2 * n_half)] + [
            pl.BlockSpec(w1.shape, lambda i: (0, 0, 0, 0)),
            pl.BlockSpec(w2.shape, lambda i: (0, 0, 0)),
            pl.BlockSpec(hid0.shape, lambda i: (0, 0, 0)),
        ],
        out_specs=[pl.BlockSpec((1, NSA_KV_GROUPS, n_str, NSA_HEAD_DIM), lambda i: (i, 0, 0, 0)),
                   pl.BlockSpec((1, NSA_KV_GROUPS, NSA_HEAD_DIM, n_str), lambda i: (i, 0, 0, 0))],
        out_shape=out,
        compiler_params=_params(("parallel",)),
        name="nsa_compress",
    )(*([p] * (2 * n_half)), w1, w2, hid0)


V_AUG = 16


def _softmax_tile_t(carry, s, vt):
    m, acc = carry
    m_new = jnp.maximum(m, jnp.max(s, axis=0, keepdims=True))
    p = jnp.exp(s - m_new).astype(BF16)
    acc = jnp.exp(m - m_new) * acc + jnp.dot(vt, p, preferred_element_type=F32)
    return m_new, acc


def _nsa_prompt_t_kernel(q_ref, gt_ref, kc_ref, vct_ref, ks_ref, vst_ref, kw_ref, vwt_ref,
                         cb_ref, bd_ref, ovt_ref, o_ref, hide_scr, *, tq, hpg, sub):
    dh = NSA_HEAD_DIM
    qi = pl.program_id(2)
    nsel = ovt_ref.shape[0]
    lanes = hpg * tq
    q = q_ref[...] * (dh ** -0.5)
    q4 = jnp.concatenate([q[:, h * dh:(h + 1) * dh] for h in range(hpg)], axis=0).astype(BF16)

    def logits(k):
        return lax.dot_general(k, q4, NT, preferred_element_type=F32)

    cb = cb_ref[0, 0]
    s = logits(kc_ref[0, 0]) + cb
    m = jnp.max(s, axis=0, keepdims=True)
    e = jnp.where(cb > VIS, jnp.exp(s - m), 0.0)
    den = jnp.sum(e, axis=0, keepdims=True)
    pc = e / jnp.where(den > 0, den, 1.0)
    oc = jnp.dot(vct_ref[0, 0], pc.astype(BF16), preferred_element_type=F32)

    pcs = pc[:, 0:tq]
    for h in range(1, hpg):
        pcs = pcs + pc[:, h * tq:(h + 1) * tq]
    imp = _dot_f32(ovt_ref[...], pcs)
    blk = lax.broadcasted_iota(jnp.int32, (nsel, tq), 0)
    tpos = qi * tq + lax.broadcasted_iota(jnp.int32, (nsel, tq), 1)
    cur = lax.shift_right_logical(tpos, int(math.log2(SEL_BLOCK)))
    forced = (blk == 0) | (blk == cur) | (blk == cur - 1)
    score = jnp.where(blk * SEL_BLOCK <= tpos, imp + FORCE_BONUS * forced.astype(F32), -1e9)
    sel = _top_k_mask(score, blk.astype(F32), 0, min(SEL_TOP_N, nsel))
    hide_scr[...] = jnp.where(sel > 0.5, 0.0, NEG)

    per_sub = tq // SEL_BLOCK
    init = (jnp.full((1, lanes), NEG, F32), jnp.zeros((dh + V_AUG, lanes), F32))

    n_back = WINDOW // tq
    ks, vs, adds = [], [], []
    for dt in range(n_back, -1, -1):
        j = qi - dt
        jc = jnp.maximum(j, 0)
        tile = 0 if dt == 0 else 1 if dt == 1 else 2 if dt == n_back else 3
        ks.append(kw_ref[0, 0, 0, pl.ds(pl.multiple_of(jc * tq, tq), tq), :])
        vs.append(vwt_ref[0, 0, 0, jc])
        adds.append(bd_ref[0, tile] + jnp.where(j >= 0, 0.0, NEG))
    s = logits(jnp.concatenate(ks, axis=0)) + jnp.concatenate(adds, axis=0)
    _, acc = _softmax_tile_t(init, s, jnp.concatenate(vs, axis=1))
    l = acc[dh:dh + 1]
    owin = acc[0:dh] / jnp.where(l > 0, l, 1.0)

    def rel_tile(rel):
        return jnp.where(rel < 0, 4, jnp.where(rel == 0, 0, jnp.where(rel == 1, 1, 3)))

    def sel_tile(j, carry):
        ks, vs, adds = [], [], []
        for s_i in range(sub):
            js = j * sub + s_i
            ks.append(ks_ref[0, 0, 0, pl.ds(pl.multiple_of(js * tq, tq), tq), :])
            vs.append(vst_ref[0, 0, 0, js])
            hide = jnp.concatenate(
                [jnp.broadcast_to(hide_scr[pl.ds(js * per_sub + b, 1), :], (SEL_BLOCK, tq)) for b in range(per_sub)],
                axis=0)
            adds.append(bd_ref[0, rel_tile(qi - js)] + jnp.concatenate([hide] * hpg, axis=1))
        s = logits(jnp.concatenate(ks, axis=0)) + jnp.concatenate(adds, axis=0)
        return _softmax_tile_t(carry, s, jnp.concatenate(vs, axis=1))

    jd = qi // sub
    carry = sel_tile(jd, init)
    _, acc = lax.fori_loop(0, jd, sel_tile, carry)
    l = acc[dh:dh + 1]
    osel = acc[0:dh] / jnp.where(l > 0, l, 1.0)

    gt = jax.nn.sigmoid(gt_ref[0])
    outs = []
    for h in range(hpg):
        blk_h = slice(h * tq, (h + 1) * tq)
        outs.append(gt[3 * h:3 * h + 1] * oc[:, blk_h] + gt[3 * h + 1:3 * h + 2] * osel[:, blk_h]
                    + gt[3 * h + 2:3 * h + 3] * owin[:, blk_h])
    o_ref[...] = jnp.concatenate(outs, axis=0).T.astype(BF16)


def nsa_prompt_t(p, gates_t, kc, vct, kt, vt, cmp_bias, diag_bias, ovt, n, t, *, tq, sub):
    g, dh = NSA_KV_GROUPS, NSA_HEAD_DIM
    nq = t // tq
    hpg = cmp_bias.shape[3] // tq
    nc = kc.shape[2]
    nsel = ovt.shape[0]
    assert t % (tq * sub) == 0 and WINDOW % tq == 0 and tq % SEL_BLOCK == 0

    def k_spec(z):
        return pl.BlockSpec((1, 1, 1, t, dh), lambda b, gi, qi, z=z: (z, b, gi, 0, 0))

    def vt_spec(z):
        return pl.BlockSpec((1, 1, 1, nq, dh + V_AUG, tq), lambda b, gi, qi, z=z: (z, b, gi, 0, 0, 0))

    return pl.pallas_call(
        functools.partial(_nsa_prompt_t_kernel, tq=tq, hpg=hpg, sub=sub),
        grid=(n, g, nq),
        in_specs=[
            pl.BlockSpec((tq, hpg * dh), lambda b, gi, qi: (b * nq + qi, gi)),
            pl.BlockSpec((1, 3 * hpg, tq), lambda b, gi, qi: (gi, 0, b * nq + qi)),
            pl.BlockSpec((1, 1, nc, dh), lambda b, gi, qi: (b, gi, 0, 0)),
            pl.BlockSpec((1, 1, dh, nc), lambda b, gi, qi: (b, gi, 0, 0)),
            k_spec(0), vt_spec(0), k_spec(1), vt_spec(1),
            pl.BlockSpec((1, 1, nc, hpg * tq), lambda b, gi, qi: (gi, qi, 0, 0)),
            pl.BlockSpec((1, 5, tq, hpg * tq), lambda b, gi, qi: (gi, 0, 0, 0)),
            pl.BlockSpec(ovt.shape, lambda b, gi, qi: (0, 0)),
        ],
        out_specs=pl.BlockSpec((tq, hpg * dh), lambda b, gi, qi: (b * nq + qi, gi)),
        out_shape=jax.ShapeDtypeStruct((n * t, g * hpg * dh), BF16),
        scratch_shapes=[pltpu.VMEM((nsel, tq), F32)],
        compiler_params=_params(("parallel", "parallel", "arbitrary")),
        name="nsa_prompt",
    )(p, gates_t, kc, vct, kt, vt, kt, vt, cmp_bias, diag_bias, ovt)


def _nsa_prompt_t_tables(rel_bias, t, tq, hpg):
    g = NSA_KV_GROUPS
    nc = t // CMP_STRIDE
    ns = -(-t // SEL_BLOCK)
    nq = t // tq

    def lanes(b):
        k = b.shape[1]
        return b.reshape(g, hpg, k, tq).transpose(0, 2, 1, 3).reshape(g, k, hpg * tq)

    tpos = jnp.arange(t)
    cmp_end = jnp.arange(nc) * CMP_STRIDE + CMP_BLOCK - 1
    dc = tpos[None, :] - cmp_end[:, None]
    cmp_bias = _dist_bias(rel_bias, dc, (dc >= 0) & (jnp.arange(nc) < nc - 1)[:, None])
    cmp_bias = cmp_bias.reshape(g, hpg, nc, nq, tq).transpose(0, 3, 2, 1, 4).reshape(g, nq, nc, hpg * tq)
    key = jnp.arange(tq)[:, None]
    qry = jnp.arange(tq)[None, :]
    d0 = qry - key
    d1 = tq + qry - key
    dw = WINDOW + qry - key
    far = jnp.full((tq, tq), 2 * REL_MAX_DIST)
    assert tq >= REL_MAX_DIST
    tiles = [
        lanes(_dist_bias(rel_bias, d0, d0 >= 0)),
        lanes(_dist_bias(rel_bias, d1, d1 >= 0)),
        lanes(_dist_bias(rel_bias, dw, dw < WINDOW)),
        lanes(_dist_bias(rel_bias, far, far >= 0)),
        lanes(_dist_bias(rel_bias, far, far < 0)),
    ]
    diag_bias = jnp.stack(tiles, axis=1)
    ovt = jnp.concatenate([_cmp_sel_overlap(nc - 1, ns), jnp.zeros((1, ns), F32)], axis=0).T
    return cmp_bias, diag_bias, ovt


def _attend_rows(tiles, q4, s_scr):
    seqs = range(len(q4))
    m = [None for _ in seqs]
    for idx in range(len(tiles[0])):
        for b in seqs:
            k, _, add, transposed = tiles[b][idx]
            kb = k.astype(BF16)
            s = (jnp.dot(q4[b], kb, preferred_element_type=F32) if transposed
                 else lax.dot_general(q4[b], kb, NT, preferred_element_type=F32)) + add
            s_scr[b, idx] = s
            mt = jnp.max(s, axis=1, keepdims=True)
            m[b] = mt if m[b] is None else jnp.maximum(m[b], mt)
    l = [None for _ in seqs]
    acc = [None for _ in seqs]
    for idx in range(len(tiles[0])):
        for b in seqs:
            _, v, _, transposed = tiles[b][idx]
            e = jnp.exp(s_scr[b, idx] - m[b])
            lt = jnp.sum(e, axis=1, keepdims=True)
            eb, vb = e.astype(BF16), v.astype(BF16)
            pv = (lax.dot_general(eb, vb, NT, preferred_element_type=F32) if transposed
                  else jnp.dot(eb, vb, preferred_element_type=F32))
            l[b] = lt if l[b] is None else l[b] + lt
            acc[b] = pv if acc[b] is None else acc[b] + pv
    return [acc[b] / jnp.where(l[b] > 0, l[b], 1.0) for b in seqs]


def _nsa_sample_compress_kernel(pt_ref, *refs, n_pages, page, nseq):
    del pt_ref
    pages = [refs[b * n_pages:(b + 1) * n_pages] for b in range(nseq)]
    w1_ref, w2_ref, b_ref, kc_ref, vc_ref, xt_scr = refs[nseq * n_pages:]
    cpt = xt_scr.shape[1]

    spp = page // CMP_STRIDE
    n_str = n_pages * spp
    n_all = nseq * n_str
    r = lax.broadcasted_iota(jnp.int32, (page, page), 0)
    t_in = lax.broadcasted_iota(jnp.int32, (page, page), 1)
    assert spp & (spp - 1) == 0
    stride_of_r = jnp.bitwise_and(r, spp - 1)
    offset_of_r = lax.shift_right_logical(r, int(math.log2(spp)))
    perm = (t_in == stride_of_r * CMP_STRIDE + offset_of_r).astype(BF16)
    for z in range(2):
        for b in range(nseq):
            for k, pg in enumerate(pages[b]):
                for c in range(cpt):
                    slab = pg[0, 0, z, pl.ds(c * LANES, LANES), :].astype(BF16)
                    rows_by_offset = lax.dot_general(perm, slab, NT, preferred_element_type=F32)
                    for i in range(CMP_STRIDE):
                        xt_scr[z, c, pl.ds(i * n_all + b * n_str + k * spp, spp), :] = \
                            rows_by_offset[i * spp:(i + 1) * spp]

    def cached(z):
        return lambda i: jnp.concatenate([xt_scr[z, c, pl.ds(i * n_all, n_all), :] for c in range(cpt)], axis=1)

    res = _compress([cached(0), cached(1)], w1_ref, w2_ref, b_ref)
    for z, dst in enumerate((kc_ref, vc_ref)):
        for b in range(nseq):
            dst[b] = res[z][b * n_str:(b + 1) * n_str].astype(BF16)


def _nsa_sample_attend_kernel(pt_ref, *refs, n_pages, page, tnew, ns, nseq):
    del pt_ref
    pages = [refs[b * n_pages:(b + 1) * n_pages] for b in range(nseq)]
    (ksn_ref, vsn_ref, kwn_ref, vwn_ref, win_ref, q4_ref, gt_ref, kc_ref, vc_ref, cb_ref, sb_ref, wb_ref,
     ov_ref, hs_ref, ex_ref, o_ref, s_scr, w_scr) = refs[nseq * n_pages:]
    g, dh = NSA_KV_GROUPS, NSA_HEAD_DIM
    gd = g * dh
    rows = q4_ref.shape[1]
    rpg = rows // g
    seqs = range(nseq)
    q4 = [q4_ref[b] for b in seqs]
    pad = jnp.zeros((page - tnew, gd), F32)

    def new_rows(ref, b):
        return jnp.concatenate([ref[pl.ds(b * tnew, tnew), :], pad], axis=0)

    w_tiles = win_ref.shape[4] // page
    tiles = []
    for b in seqs:
        tl = [(win_ref[0, b, 0, :, pl.ds(k * page, page)], win_ref[0, b, 1, :, pl.ds(k * page, page)], wb_ref[k], True)
              for k in range(w_tiles)]
        tl.append((new_rows(kwn_ref, b), new_rows(vwn_ref, b), wb_ref[w_tiles], False))
        tiles.append(tl)
    owin = _attend_rows(tiles, q4, w_scr)

    cb = cb_ref[...]
    s = [lax.dot_general(q4[b], kc_ref[b], NT, preferred_element_type=F32) + cb for b in seqs]
    e = [jnp.where(cb > VIS, jnp.exp(s[b] - jnp.max(s[b], axis=1, keepdims=True)), 0.0) for b in seqs]
    den = [jnp.sum(e[b], axis=1, keepdims=True) for b in seqs]
    pc = [e[b] / jnp.where(den[b] > 0, den[b], 1.0) for b in seqs]
    oc = [jnp.dot(pc[b].astype(BF16), vc_ref[b], preferred_element_type=F32) for b in seqs]

    imp = [_dot_f32(hs_ref[...], _dot_f32(pc[b], ov_ref[...])) for b in seqs]
    nsp = imp[0].shape[1]
    blk = lax.broadcasted_iota(jnp.int32, (rows, nsp), 1)
    cur = ns - 1
    forced = (blk == 0) | (blk == cur) | (blk == cur - 1)
    nsb = -(-ns // 8) * 8
    blk_t = blk.T[0:nsb].astype(F32)
    sel = []
    for b in seqs:
        score = jnp.where(blk < ns, imp[b] + FORCE_BONUS * forced.astype(F32), LOWEST)
        sel_t = _top_k_mask(score.T[0:nsb], blk_t, 0, min(SEL_TOP_N, ns))
        sel.append(jnp.concatenate([sel_t, jnp.zeros((nsp - nsb, rows), F32)], axis=0).T.astype(BF16))

    def hide(b, k):
        return jnp.where(jnp.dot(sel[b], ex_ref[k], preferred_element_type=F32) > 0.5, 0.0, NEG)

    tiles = []
    for b in seqs:
        tl = [(pg[0, 0, 0], pg[0, 0, 1], sb_ref[k] + hide(b, k), True) for k, pg in enumerate(pages[b])]
        tl.append((new_rows(ksn_ref, b), new_rows(vsn_ref, b), sb_ref[n_pages] + hide(b, n_pages), False))
        tiles.append(tl)
    osel = _attend_rows(tiles, q4, s_scr)

    for b in seqs:
        gt = jax.nn.sigmoid(gt_ref[b])
        o = gt[:, 0:1] * oc[b] + gt[:, 1:2] * osel[b] + gt[:, 2:3] * owin[b]
        for gi in range(g):
            o_ref[b, gi] = o[gi * rpg:(gi + 1) * rpg, gi * dh:(gi + 1) * dh].astype(BF16)


def nsa_sample(p, row0, cache_t, layer, win_t, page_table, q4, gates, tables, cweights, *, tnew):
    n, n_pages = page_table.shape
    g, dh = NSA_KV_GROUPS, NSA_HEAD_DIM
    gd = g * dh
    page = cache_t.shape[4]
    rows = q4.shape[1]
    cb, sb, wb, ov, hs, ex = tables
    w1, w2, hid0 = cweights
    ns = -(-(n_pages * page + tnew) // SEL_BLOCK)
    assert row0 % tnew == 0 and page % SEL_BLOCK == 0 and page == LANES and tnew <= page
    assert (n_pages * page + tnew) // CMP_STRIDE == n_pages * page // CMP_STRIDE
    blk0 = row0 // tnew
    q_blocks = (rows // tnew * dh) // gd
    assert gd % LANES == 0
    wbuf = win_t.shape[4]
    assert wbuf % page == 0

    nseq = 2 if n % 2 == 0 and row0 % (2 * tnew) == 0 else 1
    n_str = n_pages * page // CMP_STRIDE
    half = NSA_CACHED // 2

    def page_specs(tblk):
        return [pl.BlockSpec((1, 1, half, gd, page),
                             lambda i, pt, b=b, k=k: (layer, pt[(i * nseq + b) * n_pages + k], tblk, 0, 0))
                for b in range(nseq) for k in range(n_pages)]

    def new_spec(z):
        return pl.BlockSpec((nseq * tnew, gd), lambda i, pt, z=z: (blk0 // nseq + i, q_blocks + z))

    def const(shape):
        return pl.BlockSpec(shape, lambda i, pt: (0,) * len(shape))

    def per_seq(shape):
        return pl.BlockSpec((nseq,) + shape, lambda i, pt: (i,) + (0,) * len(shape))

    pages = [cache_t] * (nseq * n_pages)
    pt = page_table.reshape(-1)
    kc, vc = pl.pallas_call(
        functools.partial(_nsa_sample_compress_kernel, n_pages=n_pages, page=page, nseq=nseq),
        grid_spec=pltpu.PrefetchScalarGridSpec(
            num_scalar_prefetch=1,
            grid=(n // nseq,),
            in_specs=page_specs(0) + [const(w1.shape), const(w2.shape), const(hid0.shape)],
            out_specs=[per_seq((n_str, gd))] * 2,
            scratch_shapes=[pltpu.VMEM((2, gd // LANES, nseq * n_pages * page, LANES), F32)],
        ),
        out_shape=[jax.ShapeDtypeStruct((n, n_str, gd), BF16)] * 2,
        compiler_params=_params(("arbitrary",)),
        name="nsa_sample_compress",
    )(pt, *pages, w1, w2, hid0)
    return pl.pallas_call(
        functools.partial(_nsa_sample_attend_kernel, n_pages=n_pages, page=page, tnew=tnew, ns=ns, nseq=nseq),
        grid_spec=pltpu.PrefetchScalarGridSpec(
            num_scalar_prefetch=1,
            grid=(n // nseq,),
            in_specs=page_specs(1) + [
                new_spec(2), new_spec(3), new_spec(4), new_spec(5),
                pl.BlockSpec((1, nseq, 2, gd, wbuf), lambda i, pt: (layer, i, 0, 0, 0)),
                per_seq((rows, gd)), per_seq((rows, 3)), per_seq((n_str, gd)), per_seq((n_str, gd)),
                const(cb.shape), const(sb.shape), const(wb.shape), const(ov.shape), const(hs.shape),
                const(ex.shape),
            ],
            out_specs=per_seq((g, rows // g, dh)),
            scratch_shapes=[pltpu.VMEM((nseq, n_pages + 1, rows, page), F32),
                            pltpu.VMEM((nseq, wbuf // page + 1, rows, page), F32)],
        ),
        out_shape=jax.ShapeDtypeStruct((n, g, rows // g, dh), BF16),
        compiler_params=_params(("arbitrary",)),
        name="nsa_sample_attend",
    )(pt, *pages, p, p, p, p, win_t, q4, gates, kc, vc, cb, sb, wb, ov, hs, ex)


def _nsa_sample_tables(rel_bias, past, tnew, wbuf, page, hpg):
    g = NSA_KV_GROUPS
    h = g * hpg
    qpos = past + jnp.arange(tnew)
    n_str = (past + tnew) // CMP_STRIDE
    ns = -(-(past + tnew) // SEL_BLOCK)
    nsp = LANES
    assert ns <= nsp

    def rows(b):
        return b.reshape(h * tnew, b.shape[2])

    def tiled(b, n_tiles):
        return b.reshape(h * tnew, n_tiles, page).transpose(1, 0, 2)

    cmp_end = jnp.arange(n_str) * CMP_STRIDE + CMP_BLOCK - 1
    dc = qpos[:, None] - cmp_end[None, :]
    cb = rows(_dist_bias(rel_bias, dc, (dc >= 0) & (jnp.arange(n_str) < n_str - 1)[None, :]))
    n_tiles = past // page + 1
    kpos = jnp.arange(n_tiles * page)
    ds = qpos[:, None] - kpos[None, :]
    sb = tiled(rows(_dist_bias(rel_bias, ds, (ds >= 0) & (kpos < past + tnew)[None, :])), n_tiles)
    w_tiles = wbuf // page + 1
    r = jnp.arange(w_tiles * page)
    dw = qpos[:, None] - (past - wbuf + r)[None, :]
    wb = tiled(rows(_dist_bias(rel_bias, dw, (dw >= 0) & (dw < WINDOW) & (r < wbuf + tnew)[None, :])), w_tiles)
    ov = jnp.zeros((n_str, nsp), F32).at[:n_str - 1, :ns].set(_cmp_sel_overlap(n_str - 1, ns))
    row = jnp.arange(h * tnew)
    same = (row[:, None] // (hpg * tnew) == row[None, :] // (hpg * tnew)) & \
           (row[:, None] % tnew == row[None, :] % tnew)
    expand = (kpos[None, :] // SEL_BLOCK == jnp.arange(nsp)[:, None]).astype(BF16)
    expand = expand.reshape(nsp, n_tiles, page).transpose(1, 0, 2)
    return cb, sb, wb, ov, same.astype(F32), expand


def _softplus(x):
    return jnp.maximum(x, 0.0) + jnp.log1p(jnp.exp(-jnp.abs(x)))


def _transpose_rows(x):
    r = x.shape[0]
    if r >= GDN_CHUNK:
        return x.T
    return jnp.concatenate([x, jnp.zeros((LANES - r, x.shape[1]), x.dtype)], axis=0).T[:, 0:r]


def _unit_lower_inverse(a):
    c = a[0].shape[0]
    eye = (lax.broadcasted_iota(jnp.int32, (c, c), 0) == lax.broadcasted_iota(jnp.int32, (c, c), 1)).astype(F32)
    inv = [eye - x for x in a]
    power = a
    n = 2
    while n < c:
        power = [_dot(x, x) for x in power]
        inv = [y + _dot(y, x) for x, y in zip(power, inv)]
        n *= 2
    resid = [(eye - y) - _dot_split(x, y) for x, y in zip(a, inv)]
    return [y + _dot(y, r) for y, r in zip(inv, resid)]


def _gdn_kernel(qkv_ref, z_ref, ab_ref, hist_ref, s0_ref, cw_ref, nar_ref, dtr_ref, nac_ref, dtc_ref,
                nw_ref, *rest, c, tv, qk_heads, v_heads, n_prev):
    prev_refs = rest[:n_prev]
    o_ref, sfin_ref, s_scr, carry_scr = rest[n_prev:]
    hd = GDN_HEAD
    halo = carry_scr.shape[0] - tv
    step = pl.program_id(1)

    @pl.when(step == 0)
    def _():
        s_scr[...] = s0_ref[0, 0]
        carry_scr[pl.ds(0, halo), :] = hist_ref[0]

    carry_scr[pl.ds(halo, tv), :] = qkv_ref[...]
    ext = carry_scr[...]
    conv = None
    for i in range(GDN_CONV):
        shift = GDN_CONV - 1 - i
        tap = (pltpu.roll(ext, shift, axis=0) if shift else ext)[halo:halo + tv] * cw_ref[i:i + 1]
        conv = tap if conv is None else conv + tap
    carry_scr[pl.ds(0, halo), :] = carry_scr[pl.ds(tv, halo), :]
    act = conv * jax.nn.sigmoid(conv)
    if tv < c:
        act = jnp.concatenate([act, jnp.zeros((c - tv, act.shape[1]), F32)], axis=0)
        ab = jnp.concatenate([ab_ref[...], jnp.zeros((c - tv, ab_ref.shape[1]), F32)], axis=0)
    else:
        ab = ab_ref[...]

    row = lax.broadcasted_iota(jnp.int32, (c, c), 0)
    col = lax.broadcasted_iota(jnp.int32, (c, c), 1)
    incl = row >= col
    strict = row > col
    lower = incl.astype(F32)
    upper = (row <= col).astype(F32)

    live_c = lax.broadcasted_iota(jnp.int32, ab.shape, 0) < tv
    g_col = jnp.where(live_c, nar_ref[...] * _softplus(ab + dtr_ref[...]), 0.0)
    beta = jnp.where(live_c, jax.nn.sigmoid(ab), 0.0)
    abt = _transpose_rows(ab)[0:v_heads]
    live_r = lax.broadcasted_iota(jnp.int32, abt.shape, 1) < tv
    g_row = jnp.where(live_r, nac_ref[...] * _softplus(abt + dtc_ref[...]), 0.0)
    gam_col = _dot_f32(lower, g_col)
    gam_row = _dot_f32(g_row, upper)

    rep = v_heads // qk_heads
    hvs = range(v_heads)
    qn, kn, kk, qk = [], [], [], []
    for hq in range(qk_heads):
        qh = act[:, hq * hd:(hq + 1) * hd]
        kh = act[:, (qk_heads + hq) * hd:(qk_heads + hq + 1) * hd]
        qn.append(qh * lax.rsqrt(jnp.sum(qh * qh, axis=-1, keepdims=True) + 1e-6) * (hd ** -0.5))
        kn.append(kh * lax.rsqrt(jnp.sum(kh * kh, axis=-1, keepdims=True) + 1e-6))
    for hq in range(qk_heads):
        kk.append(_dot_nt(kn[hq], kn[hq]))
        qk.append(_dot_nt(qn[hq], kn[hq]))
    gc = [gam_col[:, hv:hv + 1] for hv in hvs]
    bc = [beta[:, v_heads + hv:v_heads + hv + 1] for hv in hvs]
    decay = [jnp.where(incl, jnp.exp(jnp.where(incl, gc[hv] - gam_row[hv:hv + 1, :], 0.0)), 0.0) for hv in hvs]
    a_mat = [jnp.where(strict, bc[hv] * kk[hv // rep] * decay[hv], 0.0) for hv in hvs]
    tinv = _unit_lower_inverse(a_mat)
    eg = [jnp.exp(gc[hv]) for hv in hvs]
    g_last = [gc[hv][c - 1:c, :] for hv in hvs]
    rhs = [jnp.concatenate([act[:, (2 * qk_heads + hv) * hd:(2 * qk_heads + hv + 1) * hd] * bc[hv],
                            kn[hv // rep] * (bc[hv] * eg[hv])], axis=1) for hv in hvs]
    vk = [_dot(tinv[hv], rhs[hv]) for hv in hvs]
    s_old = [s_scr[hv] for hv in hvs]
    ks = [_dot(jnp.concatenate([vk[hv][:, hd:], qn[hv // rep] * eg[hv]], axis=0), s_old[hv]) for hv in hvs]
    u = [vk[hv][:, :hd] - ks[hv][:c] for hv in hvs]
    k_dec = [_transpose_rows(kn[hv // rep] * jnp.exp(g_last[hv] - gc[hv])) for hv in hvs]
    upd = [_dot(jnp.concatenate([qk[hv // rep] * decay[hv], k_dec[hv]], axis=0), u[hv]) for hv in hvs]
    for hv in hvs:
        s_scr[hv] = s_old[hv] * jnp.exp(g_last[hv]) + upd[hv][c:]
        o = (ks[hv][c:] + upd[hv][:c])[0:tv]
        zh = z_ref[:, hv * hd:(hv + 1) * hd]
        o = _rms(o, nw_ref[...]) * (zh * jax.nn.sigmoid(zh))
        o_ref[:, hv * hd:(hv + 1) * hd] = o.astype(BF16)

    @pl.when(step == pl.num_programs(1) - 1)
    def _():
        for l, prev in enumerate(prev_refs):
            sfin_ref[l, 0] = prev[0]
        sfin_ref[n_prev, 0] = s_scr[...]


def gdn(p, row0, n, t, hist, s0, layer, conv_w, a_log, dt_bias, norm_w, *, qk_heads, v_heads, prev=()):
    hd = GDN_HEAD
    ch = (2 * qk_heads + v_heads) * hd
    vw = v_heads * hd
    tv = min(GDN_CHUNK, t)
    c = max(tv, 16)
    assert t % tv == 0 and row0 % tv == 0 and ch % vw == 0
    nchunk = t // tv
    blk0 = row0 // tv
    halo = 8
    histp = jnp.concatenate([jnp.zeros((n, halo - (GDN_CONV - 1), ch), F32), hist.astype(F32)], axis=1)
    cw = jnp.concatenate([conv_w.astype(F32), jnp.zeros((halo - GDN_CONV, ch), F32)], axis=0)
    neg_a = -jnp.exp(a_log.astype(F32))
    dt = dt_bias.astype(F32)
    pad = jnp.zeros((LANES - v_heads,), F32)
    nar = jnp.concatenate([neg_a, pad]).reshape(1, LANES)
    dtr = jnp.concatenate([dt, pad]).reshape(1, LANES)
    n_prev = len(prev)
    state_spec = pl.BlockSpec((1, v_heads, hd, hd), lambda b, s: (b, 0, 0, 0))
    o, sfin = pl.pallas_call(
        functools.partial(_gdn_kernel, c=c, tv=tv, qk_heads=qk_heads, v_heads=v_heads, n_prev=n_prev),
        grid=(n, nchunk),
        in_specs=[
            pl.BlockSpec((tv, ch), lambda b, s: (blk0 + b * nchunk + s, 0)),
            pl.BlockSpec((tv, vw), lambda b, s: (blk0 + b * nchunk + s, ch // vw)),
            pl.BlockSpec((tv, LANES), lambda b, s: (blk0 + b * nchunk + s, (ch + vw) // LANES)),
            pl.BlockSpec((1, halo, ch), lambda b, s: (b, 0, 0)),
            pl.BlockSpec((1, 1, v_heads, hd, hd), lambda b, s: (layer, b, 0, 0, 0)),
            pl.BlockSpec((halo, ch), lambda b, s: (0, 0)),
            pl.BlockSpec((1, LANES), lambda b, s: (0, 0)),
            pl.BlockSpec((1, LANES), lambda b, s: (0, 0)),
            pl.BlockSpec((v_heads, 1), lambda b, s: (0, 0)),
            pl.BlockSpec((v_heads, 1), lambda b, s: (0, 0)),
            pl.BlockSpec((1, hd), lambda b, s: (0, 0)),
        ] + [state_spec] * n_prev,
        out_specs=[
            pl.BlockSpec((tv, vw), lambda b, s: (b * nchunk + s, 0)),
            pl.BlockSpec((n_prev + 1, 1, v_heads, hd, hd), lambda b, s: (0, b, 0, 0, 0)),
        ],
        out_shape=[jax.ShapeDtypeStruct((n * t, vw), BF16),
                   jax.ShapeDtypeStruct((n_prev + 1, n, v_heads, hd, hd), F32)],
        scratch_shapes=[pltpu.VMEM((v_heads, hd, hd), F32), pltpu.VMEM((halo + tv, ch), F32)],
        compiler_params=_params(("parallel", "arbitrary")),
        name="gdn",
    )(p, p, p, histp, s0.astype(F32), cw, nar, dtr, neg_a.reshape(v_heads, 1), dt.reshape(v_heads, 1),
      norm_w.astype(F32).reshape(1, hd), *prev)
    return o, sfin


def _pow2_divisor(m, cap):
    tile = cap
    while m % tile:
        tile //= 2
    return tile


def _window(p, r0, r1, c0, c1):
    return lax.slice(p, (r0, c0), (r1, c1))


def _pad_cols(w, mult):
    n = w.shape[-1]
    return jnp.pad(w, [(0, 0)] * (w.ndim - 1) + [(0, -n % mult)])


def kernel(x_prompt, x_sample, cache_kv, cache_win, state_ssm, state_conv, page_table, rel_bias, norm_mix, norm_ffn, norm_final, nsa_w_in, nsa_w_out, nsa_cmp_w1, nsa_cmp_w2, nsa_cmp_pe, gdn_w_in, gdn_conv_w, gdn_a_log, gdn_dt_bias, gdn_norm_w, gdn_w_out, ffn_w1, ffn_w2):
    nb, t, d = x_prompt.shape
    ns_, tnew, _ = x_sample.shape
    depth = norm_mix.shape[0]
    g, dh = NSA_KV_GROUPS, NSA_HEAD_DIM
    gd = g * dh
    heads = nsa_w_out.shape[1] // dh
    hpg = heads // g
    qw = heads * dh
    mp = nb * t
    past = page_table.shape[1] * cache_kv.shape[2]
    wbuf = cache_win.shape[2]
    page = cache_kv.shape[2]
    v_heads = gdn_a_log.shape[1]
    qk_heads = (gdn_conv_w.shape[2] // GDN_HEAD - v_heads) // 2
    conv_ch = gdn_conv_w.shape[2]
    m_all = nb * t + ns_ * tnew
    tn_nsa = -(-nsa_w_in.shape[2] // LANES) * LANES
    tn_gdn = 10 * LANES
    tm_nsa = _pow2_divisor(m_all, 1024)
    tm_gdn = _pow2_divisor(m_all, 1024)
    tm_ffn = _pow2_divisor(math.gcd(mp, ns_ * tnew), 512)

    x = jnp.concatenate([x_prompt.reshape(mp, d), x_sample.reshape(ns_ * tnew, d)], axis=0).astype(F32)
    tq = LANES
    p_tables = _nsa_prompt_t_tables(rel_bias, t, tq, hpg)
    s_tables = _nsa_sample_tables(rel_bias, past, tnew, wbuf, page, hpg)
    cache_t = cache_kv.transpose(0, 1, 3, 4, 5, 2).reshape(cache_kv.shape[0], cache_kv.shape[1], NSA_CACHED, gd, page)
    win_t = cache_win.transpose(0, 1, 3, 4, 5, 2).reshape(cache_win.shape[0], ns_, 2, gd, wbuf)

    nsa_w_in_b = _pad_cols(nsa_w_in, tn_nsa).astype(BF16)
    gdn_w_in_b = _pad_cols(gdn_w_in, tn_gdn).astype(BF16)
    nsa_w_out_b, gdn_w_out_b = nsa_w_out.astype(BF16), gdn_w_out.astype(BF16)
    ffn_w1_b, ffn_w2_b = ffn_w1.astype(BF16), ffn_w2.astype(BF16)
    gains_mix = norm_mix.astype(F32).reshape(depth, 1, d)
    gains_ffn = norm_ffn.astype(F32).reshape(depth, 1, d)

    kv_p, kv_s, win_p, win_s, ssm_p, ssm_s, conv_p, conv_s = [], [], [], [], [], [], [], []
    for i in range(depth):
        li = i // 2
        if i % 2 == 0:
            p = norm_matmul(x, gains_mix, i, nsa_w_in_b, li, tm=tm_nsa, tn=tn_nsa)
            kv_end = qw + NSA_N_KV * gd
            ps = _window(p, mp, m_all, 0, kv_end + 3 * heads).reshape(ns_, tnew, -1)
            cweights = _compress_weights(nsa_cmp_w1[li], nsa_cmp_w2[li], nsa_cmp_pe[li])
            kc, vct = nsa_compress(p, nb, t, qw, *cweights)
            kv4 = _window(p, 0, mp, qw + 2 * gd, kv_end).reshape(nb, t, 2, 2, g, dh)
            kt = kv4[:, :, :, 0].transpose(2, 0, 3, 1, 4).astype(BF16)
            vt = kv4[:, :, :, 1].transpose(2, 0, 3, 4, 1).astype(BF16)
            aug = jnp.zeros((2, nb, g, V_AUG, t), BF16).at[:, :, :, 0].set(1.0)
            vt = jnp.concatenate([vt, aug], axis=3).reshape(2, nb, g, dh + V_AUG, t // tq, tq)
            vt = vt.transpose(0, 1, 2, 4, 3, 5)
            gates_p = _window(p, 0, mp, kv_end, kv_end + 3 * heads).reshape(mp, g, 3 * hpg).transpose(1, 2, 0)
            o_p = nsa_prompt_t(p, gates_p, kc, vct, kt, vt, *p_tables, nb, t, tq=tq, sub=4)
            q_s = (ps[:, :, :qw] * dh ** -0.5).reshape(ns_, tnew, g, hpg, dh)
            q4 = jnp.einsum("ntghd,gk->nghtkd", q_s, jnp.eye(g, dtype=F32))
            q4 = q4.reshape(ns_, heads * tnew, gd).astype(BF16)
            gates_s = ps[:, :, kv_end:kv_end + 3 * heads].reshape(ns_, tnew, heads, 3)
            gates_s = gates_s.transpose(0, 2, 1, 3).reshape(ns_, heads * tnew, 3)
            o_s = nsa_sample(p, mp, cache_t, li, win_t, page_table, q4, gates_s, s_tables, cweights, tnew=tnew)
            o_s = o_s.reshape(ns_, g, hpg, tnew, dh).transpose(0, 3, 1, 2, 4).reshape(ns_ * tnew, qw)
            w_outs = nsa_w_out_b
            kv_p.append(_window(p, 0, mp, qw, qw + NSA_CACHED * gd).reshape(nb, t, NSA_CACHED, g, dh))
            kv_s.append(ps[:, :, qw:qw + NSA_CACHED * gd].reshape(ns_, tnew, NSA_CACHED, g, dh))
            wkeep = min(WINDOW, t)
            win_p.append(jnp.stack([_window(p, b * t + t - wkeep, (b + 1) * t, qw + NSA_CACHED * gd, kv_end)
                                    for b in range(nb)]).reshape(nb, wkeep, 2, g, dh))
            win_s.append(ps[:, :, qw + NSA_CACHED * gd:kv_end].reshape(ns_, tnew, 2, g, dh))
        else:
            p = norm_matmul(x, gains_mix, i, gdn_w_in_b, li, tm=tm_gdn, tn=tn_gdn)
            args = (gdn_conv_w[li], gdn_a_log[li], gdn_dt_bias[li], gdn_norm_w[li])
            h0 = jnp.zeros((nb, GDN_CONV - 1, conv_ch), F32)
            s0 = jnp.zeros((1, nb, v_heads, GDN_HEAD, GDN_HEAD), F32)
            last = i + 2 >= depth
            o_p, sp = gdn(p, 0, nb, t, h0, s0, 0, *args, qk_heads=qk_heads, v_heads=v_heads,
                          prev=tuple(ssm_p) if last else ())
            o_s, ss = gdn(p, mp, ns_, tnew, state_conv[li], state_ssm, li, *args,
                          qk_heads=qk_heads, v_heads=v_heads, prev=tuple(ssm_s) if last else ())
            sp, ss = (sp, ss) if last else (sp[0], ss[0])
            w_outs = gdn_w_out_b
            ssm_p.append(sp.astype(state_ssm.dtype))
            ssm_s.append(ss.astype(state_ssm.dtype))
            keep = GDN_CONV - 1
            assert t >= keep and tnew >= keep
            conv_p.append(jnp.stack([_window(p, (b + 1) * t - keep, (b + 1) * t, 0, conv_ch) for b in range(nb)]))
            conv_s.append(_window(p, mp, m_all, 0, conv_ch).reshape(ns_, tnew, conv_ch)[:, tnew - keep:])
        x = mix_ffn(x, o_p, o_s, w_outs, li, gains_ffn, ffn_w1_b, ffn_w2_b, i, norm_final,
                    final_norm=(i == depth - 1), tm=tm_ffn, th=1024)
    y_prompt = x[:mp].reshape(nb, t, d)
    y_sample = x[mp:].reshape(ns_, tnew, d)
    new_win = jnp.stack(win_s).astype(cache_win.dtype)
    win_s_all = jnp.concatenate([cache_win[:new_win.shape[0]], new_win], axis=2)[:, :, -wbuf:]
    return (y_prompt, y_sample, jnp.stack(kv_p), jnp.stack(kv_s), jnp.stack(win_p), win_s_all,
            ssm_p[-1], ssm_s[-1], jnp.stack(conv_p), jnp.stack(conv_s))
```

```python
import functools
import math

import jax
import jax.numpy as jnp
from jax import lax
from jax.experimental import pallas as pl
from jax.experimental.pallas import tpu as pltpu

F32 = jnp.float32
BF16 = jnp.bfloat16
HIGHEST = lax.Precision.HIGHEST

LANES = 128
VMEM_LIMIT = 56 * 1024 * 1024

NSA_HEAD_DIM = 64
NSA_KV_GROUPS = 4
NSA_N_KV = 6
NSA_CACHED = 4
CMP_BLOCK = 32
CMP_STRIDE = 16
SEL_BLOCK = 64
SEL_TOP_N = 8
WINDOW = 512
FORCE_BONUS = 1e4
REL_BUCKETS = 32
REL_MAX_DIST = 128
GDN_HEAD = 128
GDN_CONV = 4
GDN_CHUNK = 64
RMS_EPS = 1e-6
NEG = -1e30
VIS = -1e29
LOWEST = -3e38

NT = (((1,), (1,)), ((), ()))


def _params(sem):
    return pltpu.CompilerParams(dimension_semantics=sem, vmem_limit_bytes=VMEM_LIMIT)


def _dot(a, b):
    return jnp.dot(a.astype(BF16), b.astype(BF16), preferred_element_type=F32)


def _dot_nt(a, b):
    return lax.dot_general(a.astype(BF16), b.astype(BF16), NT, preferred_element_type=F32)


def _dot_split(a, b):
    ah, bh = a.astype(BF16), b.astype(BF16)
    al = (a - ah.astype(F32)).astype(BF16)
    bl = (b - bh.astype(F32)).astype(BF16)
    hh = jnp.dot(ah, bh, preferred_element_type=F32)
    return hh + (jnp.dot(ah, bl, preferred_element_type=F32) + jnp.dot(al, bh, preferred_element_type=F32))


def _dot_f32(a, b):
    return jnp.dot(a, b, precision=HIGHEST, preferred_element_type=F32)


def _rms(x, gain):
    ms = jnp.mean(x * x, axis=-1, keepdims=True)
    return x * lax.rsqrt(ms + RMS_EPS) * gain


def _norm_matmul_kernel(x_ref, g_ref, w_ref, o_ref, xn_ref):
    @pl.when(pl.program_id(1) == 0)
    def _():
        xn_ref[...] = _rms(x_ref[...], g_ref[...]).astype(BF16)

    o_ref[...] = jnp.dot(xn_ref[...], w_ref[...], preferred_element_type=F32)


def norm_matmul(x, gains, gi, ws, wi, *, tm, tn):
    m, d = x.shape
    n = ws.shape[2]
    assert m % tm == 0 and n % tn == 0
    return pl.pallas_call(
        _norm_matmul_kernel,
        grid=(m // tm, n // tn),
        in_specs=[
            pl.BlockSpec((tm, d), lambda i, j: (i, 0)),
            pl.BlockSpec((None, 1, d), lambda i, j: (gi, 0, 0)),
            pl.BlockSpec((None, d, tn), lambda i, j: (wi, 0, j)),
        ],
        out_specs=pl.BlockSpec((tm, tn), lambda i, j: (i, j)),
        out_shape=jax.ShapeDtypeStruct((m, n), F32),
        scratch_shapes=[pltpu.VMEM((tm, d), BF16)],
        compiler_params=_params(("parallel", "arbitrary")),
        name="norm_matmul",
    )(x, gains, ws)


def _mix_ffn_kernel(x_ref, oa_ref, ob_ref, wo_ref, g_ref, w1_ref, w2_ref, gf_ref, y_ref,
                    x1_ref, xn_ref, acc_ref, *, final_norm, tiles_a):
    j = pl.program_id(1)

    @pl.when(j == 0)
    def _():
        o = jnp.where(pl.program_id(0) < tiles_a, oa_ref[...], ob_ref[...])
        x1 = x_ref[...] + jnp.dot(o, wo_ref[...], preferred_element_type=F32)
        x1_ref[...] = x1
        xn_ref[...] = _rms(x1, g_ref[...]).astype(BF16)
        acc_ref[...] = jnp.zeros_like(acc_ref)

    hid = jnp.maximum(jnp.dot(xn_ref[...], w1_ref[...], preferred_element_type=F32), 0.0)
    acc_ref[...] += jnp.dot((hid * hid).astype(BF16), w2_ref[...], preferred_element_type=F32)

    @pl.when(j == pl.num_programs(1) - 1)
    def _():
        y = x1_ref[...] + acc_ref[...]
        if final_norm:
            y = _rms(y, gf_ref[...])
        y_ref[...] = y


def mix_ffn(x, o_a, o_b, w_outs, wo_i, gains, w1s, w2s, layer, gain_final, *, final_norm, tm, th):
    m, d = x.shape
    ko = o_a.shape[1]
    hdim = w1s.shape[2]
    assert m % tm == 0 and hdim % th == 0 and o_a.shape[0] % tm == 0 and o_a.shape[0] + o_b.shape[0] == m
    tiles_a = o_a.shape[0] // tm
    return pl.pallas_call(
        functools.partial(_mix_ffn_kernel, final_norm=final_norm, tiles_a=tiles_a),
        grid=(m // tm, hdim // th),
        in_specs=[
            pl.BlockSpec((tm, d), lambda i, j: (i, 0)),
            pl.BlockSpec((tm, ko), lambda i, j: (jnp.minimum(i, tiles_a - 1), 0)),
            pl.BlockSpec((tm, ko), lambda i, j: (jnp.maximum(i - tiles_a, 0), 0)),
            pl.BlockSpec((None, ko, d), lambda i, j: (wo_i, 0, 0)),
            pl.BlockSpec((None, 1, d), lambda i, j: (layer, 0, 0)),
            pl.BlockSpec((None, d, th), lambda i, j: (layer, 0, j)),
            pl.BlockSpec((None, th, d), lambda i, j: (layer, j, 0)),
            pl.BlockSpec((1, d), lambda i, j: (0, 0)),
        ],
        out_specs=pl.BlockSpec((tm, d), lambda i, j: (i, 0)),
        out_shape=jax.ShapeDtypeStruct((m, d), F32),
        scratch_shapes=[pltpu.VMEM((tm, d), F32), pltpu.VMEM((tm, d), BF16), pltpu.VMEM((tm, d), F32)],
        compiler_params=_params(("parallel", "arbitrary")),
        name="mix_ffn",
    )(x, o_a, o_b, w_outs, gains, w1s, w2s, gain_final.reshape(1, d))


def _rel_bucket(dist):
    n = jnp.maximum(dist, 0)
    max_exact = REL_BUCKETS // 2
    nf = jnp.maximum(n, 1).astype(F32)
    large = max_exact + (jnp.log(nf / max_exact) / math.log(REL_MAX_DIST / max_exact)
                         * (REL_BUCKETS - max_exact)).astype(jnp.int32)
    return jnp.where(n < max_exact, n, jnp.minimum(large, REL_BUCKETS - 1))


def _dist_bias(rel_bias, dist, visible):
    tbl = rel_bias.astype(F32)
    onehot = (_rel_bucket(dist)[..., None] == jnp.arange(REL_BUCKETS)).astype(F32)
    b = jnp.einsum("...k,kh->h...", onehot, tbl, precision=HIGHEST)
    return jnp.where(visible[None], b, NEG)


def _cmp_sel_overlap(nc, ns):
    c0 = jnp.arange(nc) * CMP_STRIDE
    s0 = jnp.arange(ns) * SEL_BLOCK
    ov = jnp.minimum(c0[:, None] + CMP_BLOCK, s0[None, :] + SEL_BLOCK) - jnp.maximum(c0[:, None], s0[None, :])
    return jnp.maximum(ov, 0).astype(F32) / CMP_BLOCK


def _compress_weights(cw1, cw2, cpe):
    g = NSA_KV_GROUPS
    eye = jnp.eye(g, dtype=F32)
    w1 = jnp.einsum("zide,gh->zigdhe", cw1.astype(F32), eye)
    w1 = w1.reshape(2, CMP_BLOCK, g * NSA_HEAD_DIM, g * NSA_HEAD_DIM).astype(BF16)
    w2 = jnp.einsum("zde,gh->zgdhe", cw2.astype(F32), eye)
    w2 = w2.reshape(2, g * NSA_HEAD_DIM, g * NSA_HEAD_DIM).astype(BF16)
    hid0 = jnp.einsum("zid,zide->ze", cpe.astype(F32), cw1.astype(F32), precision=HIGHEST)
    hid0 = jnp.tile(hid0[:, None, :], (1, 1, g))
    return w1, w2, hid0


def _compress(load_rows, w1_ref, w2_ref, b_ref):
    r = CMP_BLOCK // CMP_STRIDE
    assert r == 2
    types = range(len(load_rows))
    acc_a = [None for _ in types]
    acc_b = [None for _ in types]
    for i in range(CMP_STRIDE):
        for z in types:
            xi = load_rows[z](i).astype(BF16)
            da = jnp.dot(xi, w1_ref[z, i], preferred_element_type=F32)
            db = jnp.dot(xi, w1_ref[z, CMP_STRIDE + i], preferred_element_type=F32)
            acc_a[z] = da if acc_a[z] is None else acc_a[z] + da
            acc_b[z] = db if acc_b[z] is None else acc_b[z] + db
    n_str = acc_a[0].shape[0]
    out = []
    for z in types:
        hid = acc_a[z] + pltpu.roll(acc_b[z], n_str - 1, axis=0) + b_ref[z]
        act = hid * jax.nn.sigmoid(hid)
        out.append(jnp.dot(act.astype(BF16), w2_ref[z], preferred_element_type=F32))
    return out


def _top_k_mask(score, index, axis, k):
    sel = jnp.zeros(score.shape, F32)
    for _ in range(k):
        mx = jnp.max(score, axis=axis, keepdims=True)
        cand = jnp.where(score == mx, index, 1e9)
        first = jnp.min(cand, axis=axis, keepdims=True)
        hit = index == first
        sel = jnp.where(hit, 1.0, sel)
        score = jnp.where(hit, LOWEST, score)
    return sel


def _nsa_compress_kernel(*refs, n_str, n_half):
    g, dh = NSA_KV_GROUPS, NSA_HEAD_DIM
    srcs = (refs[:n_half], refs[n_half:2 * n_half])
    w1_ref, w2_ref, b_ref, kc_ref, vc_ref = refs[2 * n_half:]
    def load_rows(z):
        return lambda i: jnp.concatenate([r[pl.ds(i, n_str, stride=CMP_STRIDE), :] for r in srcs[z]], axis=1)

    kc, vc = _compress([load_rows(0), load_rows(1)], w1_ref, w2_ref, b_ref)
    vct = vc.T
    for gi in range(g):
        kc_ref[0, gi] = kc[:, gi * dh:(gi + 1) * dh].astype(BF16)
        vc_ref[0, gi] = vct[gi * dh:(gi + 1) * dh].astype(BF16)


def nsa_compress(p, n, t, qw, w1, w2, hid0):
    gd = NSA_KV_GROUPS * NSA_HEAD_DIM
    n_str = t // CMP_STRIDE
    assert qw % LANES == 0 and gd % LANES == 0
    n_half = gd // LANES
    q_blocks = qw // LANES
    out = [jax.ShapeDtypeStruct((n, NSA_KV_GROUPS, n_str, NSA_HEAD_DIM), BF16),
           jax.ShapeDtypeStruct((n, NSA_KV_GROUPS, NSA_HEAD_DIM, n_str), BF16)]
    return pl.pallas_call(
        functools.partial(_nsa_compress_kernel, n_str=n_str, n_half=n_half),
        grid=(n,),
        in_specs=[pl.BlockSpec((t, LANES), lambda i, c=c: (i, q_blocks + c)) for c in range(2 * n_half)] + [
            pl.BlockSpec(w1.shape, lambda i: (0, 0, 0, 0)),
            pl.BlockSpec(w2.shape, lambda i: (0, 0, 0)),
            pl.BlockSpec(hid0.shape, lambda i: (0, 0, 0)),
        ],
        out_specs=[pl.BlockSpec((1, NSA_KV_GROUPS, n_str, NSA_HEAD_DIM), lambda i: (i, 0, 0, 0)),
                   pl.BlockSpec((1, NSA_KV_GROUPS, NSA_HEAD_DIM, n_str), lambda i: (i, 0, 0, 0))],
        out_shape=out,
        compiler_params=_params(("parallel",)),
        name="nsa_compress",
    )(*([p] * (2 * n_half)), w1, w2, hid0)


V_AUG = 16


def _softmax_tile_t(carry, s, vt):
    m, acc = carry
    m_new = jnp.maximum(m, jnp.max(s, axis=0, keepdims=True))
    p = jnp.exp(s - m_new).astype(BF16)
    acc = jnp.exp(m - m_new) * acc + jnp.dot(vt, p, preferred_element_type=F32)
    return m_new, acc


def _nsa_prompt_t_kernel(q_ref, gt_ref, kc_ref, vct_ref, ks_ref, vst_ref, kw_ref, vwt_ref,
                         cb_ref, bd_ref, ovt_ref, o_ref, hide_scr, *, tq, hpg, sub):
    dh = NSA_HEAD_DIM
    qi = pl.program_id(2)
    nsel = ovt_ref.shape[0]
    lanes = hpg * tq
    q = q_ref[...] * (dh ** -0.5)
    q4 = jnp.concatenate([q[:, h * dh:(h + 1) * dh] for h in range(hpg)], axis=0).astype(BF16)

    def logits(k):
        return lax.dot_general(k, q4, NT, preferred_element_type=F32)

    cb = cb_ref[0, 0]
    s = logits(kc_ref[0, 0]) + cb
    m = jnp.max(s, axis=0, keepdims=True)
    e = jnp.where(cb > VIS, jnp.exp(s - m), 0.0)
    den = jnp.sum(e, axis=0, keepdims=True)
    pc = e / jnp.where(den > 0, den, 1.0)
    oc = jnp.dot(vct_ref[0, 0], pc.astype(BF16), preferred_element_type=F32)

    pcs = pc[:, 0:tq]
    for h in range(1, hpg):
        pcs = pcs + pc[:, h * tq:(h + 1) * tq]
    imp = _dot_f32(ovt_ref[...], pcs)
    blk = lax.broadcasted_iota(jnp.int32, (nsel, tq), 0)
    tpos = qi * tq + lax.broadcasted_iota(jnp.int32, (nsel, tq), 1)
    cur = lax.shift_right_logical(tpos, int(math.log2(SEL_BLOCK)))
    forced = (blk == 0) | (blk == cur) | (blk == cur - 1)
    score = jnp.where(blk * SEL_BLOCK <= tpos, imp + FORCE_BONUS * forced.astype(F32), -1e9)
    sel = _top_k_mask(score, blk.astype(F32), 0, min(SEL_TOP_N, nsel))
    hide_scr[...] = jnp.where(sel > 0.5, 0.0, NEG)

    per_sub = tq // SEL_BLOCK
    init = (jnp.full((1, lanes), NEG, F32), jnp.zeros((dh + V_AUG, lanes), F32))

    n_back = WINDOW // tq
    ks, vs, adds = [], [], []
    for dt in range(n_back, -1, -1):
        j = qi - dt
        jc = jnp.maximum(j, 0)
        tile = 0 if dt == 0 else 1 if dt == 1 else 2 if dt == n_back else 3
        ks.append(kw_ref[0, 0, 0, pl.ds(pl.multiple_of(jc * tq, tq), tq), :])
        vs.append(vwt_ref[0, 0, 0, jc])
        adds.append(bd_ref[0, tile] + jnp.where(j >= 0, 0.0, NEG))
    s = logits(jnp.concatenate(ks, axis=0)) + jnp.concatenate(adds, axis=0)
    _, acc = _softmax_tile_t(init, s, jnp.concatenate(vs, axis=1))
    l = acc[dh:dh + 1]
    owin = acc[0:dh] / jnp.where(l > 0, l, 1.0)

    def rel_tile(rel):
        return jnp.where(rel < 0, 4, jnp.where(rel == 0, 0, jnp.where(rel == 1, 1, 3)))

    def sel_tile(j, carry):
        ks, vs, adds = [], [], []
        for s_i in range(sub):
            js = j * sub + s_i
            ks.append(ks_ref[0, 0, 0, pl.ds(pl.multiple_of(js * tq, tq), tq), :])
            vs.append(vst_ref[0, 0, 0, js])
            hide = jnp.concatenate(
                [jnp.broadcast_to(hide_scr[pl.ds(js * per_sub + b, 1), :], (SEL_BLOCK, tq)) for b in range(per_sub)],
                axis=0)
            adds.append(bd_ref[0, rel_tile(qi - js)] + jnp.concatenate([hide] * hpg, axis=1))
        s = logits(jnp.concatenate(ks, axis=0)) + jnp.concatenate(adds, axis=0)
        return _softmax_tile_t(carry, s, jnp.concatenate(vs, axis=1))

    jd = qi // sub
    carry = sel_tile(jd, init)
    _, acc = lax.fori_loop(0, jd, sel_tile, carry)
    l = acc[dh:dh + 1]
    osel = acc[0:dh] / jnp.where(l > 0, l, 1.0)

    gt = jax.nn.sigmoid(gt_ref[0])
    outs = []
    for h in range(hpg):
        blk_h = slice(h * tq, (h + 1) * tq)
        outs.append(gt[3 * h:3 * h + 1] * oc[:, blk_h] + gt[3 * h + 1:3 * h + 2] * osel[:, blk_h]
                    + gt[3 * h + 2:3 * h + 3] * owin[:, blk_h])
    o_ref[...] = jnp.concatenate(outs, axis=0).T.astype(BF16)


def nsa_prompt_t(p, gates_t, kc, vct, kt, vt, cmp_bias, diag_bias, ovt, n, t, *, tq, sub):
    g, dh = NSA_KV_GROUPS, NSA_HEAD_DIM
    nq = t // tq
    hpg = cmp_bias.shape[3] // tq
    nc = kc.shape[2]
    nsel = ovt.shape[0]
    assert t % (tq * sub) == 0 and WINDOW % tq == 0 and tq % SEL_BLOCK == 0

    def k_spec(z):
        return pl.BlockSpec((1, 1, 1, t, dh), lambda b, gi, qi, z=z: (z, b, gi, 0, 0))

    def vt_spec(z):
        return pl.BlockSpec((1, 1, 1, nq, dh + V_AUG, tq), lambda b, gi, qi, z=z: (z, b, gi, 0, 0, 0))

    return pl.pallas_call(
        functools.partial(_nsa_prompt_t_kernel, tq=tq, hpg=hpg, sub=sub),
        grid=(n, g, nq),
        in_specs=[
            pl.BlockSpec((tq, hpg * dh), lambda b, gi, qi: (b * nq + qi, gi)),
            pl.BlockSpec((1, 3 * hpg, tq), lambda b, gi, qi: (gi, 0, b * nq + qi)),
            pl.BlockSpec((1, 1, nc, dh), lambda b, gi, qi: (b, gi, 0, 0)),
            pl.BlockSpec((1, 1, dh, nc), lambda b, gi, qi: (b, gi, 0, 0)),
            k_spec(0), vt_spec(0), k_spec(1), vt_spec(1),
            pl.BlockSpec((1, 1, nc, hpg * tq), lambda b, gi, qi: (gi, qi, 0, 0)),
            pl.BlockSpec((1, 5, tq, hpg * tq), lambda b, gi, qi: (gi, 0, 0, 0)),
            pl.BlockSpec(ovt.shape, lambda b, gi, qi: (0, 0)),
        ],
        out_specs=pl.BlockSpec((tq, hpg * dh), lambda b, gi, qi: (b * nq + qi, gi)),
        out_shape=jax.ShapeDtypeStruct((n * t, g * hpg * dh), BF16),
        scratch_shapes=[pltpu.VMEM((nsel, tq), F32)],
        compiler_params=_params(("parallel", "parallel", "arbitrary")),
        name="nsa_prompt",
    )(p, gates_t, kc, vct, kt, vt, kt, vt, cmp_bias, diag_bias, ovt)


def _nsa_prompt_t_tables(rel_bias, t, tq, hpg):
    g = NSA_KV_GROUPS
    nc = t // CMP_STRIDE
    ns = -(-t // SEL_BLOCK)
    nq = t // tq

    def lanes(b):
        k = b.shape[1]
        return b.reshape(g, hpg, k, tq).transpose(0, 2, 1, 3).reshape(g, k, hpg * tq)

    tpos = jnp.arange(t)
    cmp_end = jnp.arange(nc) * CMP_STRIDE + CMP_BLOCK - 1
    dc = tpos[None, :] - cmp_end[:, None]
    cmp_bias = _dist_bias(rel_bias, dc, (dc >= 0) & (jnp.arange(nc) < nc - 1)[:, None])
    cmp_bias = cmp_bias.reshape(g, hpg, nc, nq, tq).transpose(0, 3, 2, 1, 4).reshape(g, nq, nc, hpg * tq)
    key = jnp.arange(tq)[:, None]
    qry = jnp.arange(tq)[None, :]
    d0 = qry - key
    d1 = tq + qry - key
    dw = WINDOW + qry - key
    far = jnp.full((tq, tq), 2 * REL_MAX_DIST)
    assert tq >= REL_MAX_DIST
    tiles = [
        lanes(_dist_bias(rel_bias, d0, d0 >= 0)),
        lanes(_dist_bias(rel_bias, d1, d1 >= 0)),
        lanes(_dist_bias(rel_bias, dw, dw < WINDOW)),
        lanes(_dist_bias(rel_bias, far, far >= 0)),
        lanes(_dist_bias(rel_bias, far, far < 0)),
    ]
    diag_bias = jnp.stack(tiles, axis=1)
    ovt = jnp.concatenate([_cmp_sel_overlap(nc - 1, ns), jnp.zeros((1, ns), F32)], axis=0).T
    return cmp_bias, diag_bias, ovt


def _attend_rows(tiles, q4, s_scr):
    seqs = range(len(q4))
    m = [None for _ in seqs]
    for idx in range(len(tiles[0])):
        for b in seqs:
            k, _, add, transposed = tiles[b][idx]
            kb = k.astype(BF16)
            s = (jnp.dot(q4[b], kb, preferred_element_type=F32) if transposed
                 else lax.dot_general(q4[b], kb, NT, preferred_element_type=F32)) + add
            s_scr[b, idx] = s
            mt = jnp.max(s, axis=1, keepdims=True)
            m[b] = mt if m[b] is None else jnp.maximum(m[b], mt)
    l = [None for _ in seqs]
    acc = [None for _ in seqs]
    for idx in range(len(tiles[0])):
        for b in seqs:
            _, v, _, transposed = tiles[b][idx]
            e = jnp.exp(s_scr[b, idx] - m[b])
            lt = jnp.sum(e, axis=1, keepdims=True)
            eb, vb = e.astype(BF16), v.astype(BF16)
            pv = (lax.dot_general(eb, vb, NT, preferred_element_type=F32) if transposed
                  else jnp.dot(eb, vb, preferred_element_type=F32))
            l[b] = lt if l[b] is None else l[b] + lt
            acc[b] = pv if acc[b] is None else acc[b] + pv
    return [acc[b] / jnp.where(l[b] > 0, l[b], 1.0) for b in seqs]


def _nsa_sample_compress_kernel(pt_ref, *refs, n_pages, page, nseq):
    del pt_ref
    pages = [refs[b * n_pages:(b + 1) * n_pages] for b in range(nseq)]
    w1_ref, w2_ref, b_ref, kc_ref, vc_ref, xt_scr = refs[nseq * n_pages:]
    cpt = xt_scr.shape[1]

    spp = page // CMP_STRIDE
    n_str = n_pages * spp
    n_all = nseq * n_str
    assert spp & (spp - 1) == 0
    tok = lax.broadcasted_iota(jnp.int32, (page, page), 0)
    row = lax.broadcasted_iota(jnp.int32, (page, page), 1)
    perm_t = (tok == jnp.bitwise_and(row, spp - 1) * CMP_STRIDE
              + lax.shift_right_logical(row, int(math.log2(spp)))).astype(BF16)
    for z in range(2):
        for b in range(nseq):
            for k, pg in enumerate(pages[b]):
                for c in range(cpt):
                    slab = pg[0, 0, z, pl.ds(c * LANES, LANES), :].astype(BF16)
                    rows_by_offset = jnp.dot(slab, perm_t, preferred_element_type=F32).T
                    for i in range(CMP_STRIDE):
                        xt_scr[z, c, pl.ds(i * n_all + b * n_str + k * spp, spp), :] = \
                            rows_by_offset[i * spp:(i + 1) * spp]

    def cached(z):
        return lambda i: jnp.concatenate([xt_scr[z, c, pl.ds(i * n_all, n_all), :] for c in range(cpt)], axis=1)

    res = _compress([cached(0), cached(1)], w1_ref, w2_ref, b_ref)
    for z, dst in enumerate((kc_ref, vc_ref)):
        for b in range(nseq):
            dst[b] = res[z][b * n_str:(b + 1) * n_str].astype(BF16)


def _nsa_sample_attend_kernel(pt_ref, *refs, n_pages, page, tnew, ns, nseq):
    del pt_ref
    pages = [refs[b * n_pages:(b + 1) * n_pages] for b in range(nseq)]
    (ksn_ref, vsn_ref, kwn_ref, vwn_ref, win_ref, q4_ref, gt_ref, kc_ref, vc_ref, cb_ref, sb_ref, wb_ref,
     ov_ref, hs_ref, ex_ref, o_ref, s_scr, w_scr) = refs[nseq * n_pages:]
    g, dh = NSA_KV_GROUPS, NSA_HEAD_DIM
    gd = g * dh
    rows = q4_ref.shape[1]
    rpg = rows // g
    seqs = range(nseq)
    q4 = [q4_ref[b] for b in seqs]
    pad = jnp.zeros((page - tnew, gd), F32)

    def new_rows(ref, b):
        return jnp.concatenate([ref[pl.ds(b * tnew, tnew), :], pad], axis=0)

    w_tiles = win_ref.shape[4] // page
    tiles = []
    for b in seqs:
        tl = [(win_ref[0, b, 0, :, pl.ds(k * page, page)], win_ref[0, b, 1, :, pl.ds(k * page, page)], wb_ref[k], True)
              for k in range(w_tiles)]
        tl.append((new_rows(kwn_ref, b), new_rows(vwn_ref, b), wb_ref[w_tiles], False))
        tiles.append(tl)
    owin = _attend_rows(tiles, q4, w_scr)

    cb = cb_ref[...]
    s = [lax.dot_general(q4[b], kc_ref[b], NT, preferred_element_type=F32) + cb for b in seqs]
    e = [jnp.where(cb > VIS, jnp.exp(s[b] - jnp.max(s[b], axis=1, keepdims=True)), 0.0) for b in seqs]
    den = [jnp.sum(e[b], axis=1, keepdims=True) for b in seqs]
    pc = [e[b] / jnp.where(den[b] > 0, den[b], 1.0) for b in seqs]
    oc = [jnp.dot(pc[b].astype(BF16), vc_ref[b], preferred_element_type=F32) for b in seqs]

    imp = [_dot_f32(hs_ref[...], _dot_f32(pc[b], ov_ref[...])) for b in seqs]
    nsp = imp[0].shape[1]
    blk = lax.broadcasted_iota(jnp.int32, (rows, nsp), 1)
    cur = ns - 1
    forced = (blk == 0) | (blk == cur) | (blk == cur - 1)
    nsb = -(-ns // 8) * 8
    blk_t = blk.T[0:nsb].astype(F32)
    sel = []
    for b in seqs:
        score = jnp.where(blk < ns, imp[b] + FORCE_BONUS * forced.astype(F32), LOWEST)
        sel_t = _top_k_mask(score.T[0:nsb], blk_t, 0, min(SEL_TOP_N, ns))
        sel.append(jnp.concatenate([sel_t, jnp.zeros((nsp - nsb, rows), F32)], axis=0).T.astype(BF16))

    def hide(b, k):
        return jnp.where(jnp.dot(sel[b], ex_ref[k], preferred_element_type=F32) > 0.5, 0.0, NEG)

    tiles = []
    for b in seqs:
        tl = [(pg[0, 0, 0], pg[0, 0, 1], sb_ref[k] + hide(b, k), True) for k, pg in enumerate(pages[b])]
        tl.append((new_rows(ksn_ref, b), new_rows(vsn_ref, b), sb_ref[n_pages] + hide(b, n_pages), False))
        tiles.append(tl)
    osel = _attend_rows(tiles, q4, s_scr)

    for b in seqs:
        gt = jax.nn.sigmoid(gt_ref[b])
        o = gt[:, 0:1] * oc[b] + gt[:, 1:2] * osel[b] + gt[:, 2:3] * owin[b]
        for gi in range(g):
            o_ref[b, gi] = o[gi * rpg:(gi + 1) * rpg, gi * dh:(gi + 1) * dh].astype(BF16)


def nsa_sample(p, row0, cache_t, layer, win_t, page_table, q4, gates, tables, cweights, *, tnew):
    n, n_pages = page_table.shape
    g, dh = NSA_KV_GROUPS, NSA_HEAD_DIM
    gd = g * dh
    page = cache_t.shape[4]
    rows = q4.shape[1]
    cb, sb, wb, ov, hs, ex = tables
    w1, w2, hid0 = cweights
    ns = -(-(n_pages * page + tnew) // SEL_BLOCK)
    assert row0 % tnew == 0 and page % SEL_BLOCK == 0 and page == LANES and tnew <= page
    assert (n_pages * page + tnew) // CMP_STRIDE == n_pages * page // CMP_STRIDE
    blk0 = row0 // tnew
    q_blocks = (rows // tnew * dh) // gd
    assert gd % LANES == 0
    wbuf = win_t.shape[4]
    assert wbuf % page == 0

    nseq = 2 if n % 2 == 0 and row0 % (2 * tnew) == 0 else 1
    n_str = n_pages * page // CMP_STRIDE
    half = NSA_CACHED // 2

    def page_specs(tblk):
        return [pl.BlockSpec((1, 1, half, gd, page),
                             lambda i, pt, b=b, k=k: (layer, pt[(i * nseq + b) * n_pages + k], tblk, 0, 0))
                for b in range(nseq) for k in range(n_pages)]

    def new_spec(z):
        return pl.BlockSpec((nseq * tnew, gd), lambda i, pt, z=z: (blk0 // nseq + i, q_blocks + z))

    def const(shape):
        return pl.BlockSpec(shape, lambda i, pt: (0,) * len(shape))

    def per_seq(shape):
        return pl.BlockSpec((nseq,) + shape, lambda i, pt: (i,) + (0,) * len(shape))

    pages = [cache_t] * (nseq * n_pages)
    pt = page_table.reshape(-1)
    kc, vc = pl.pallas_call(
        functools.partial(_nsa_sample_compress_kernel, n_pages=n_pages, page=page, nseq=nseq),
        grid_spec=pltpu.PrefetchScalarGridSpec(
            num_scalar_prefetch=1,
            grid=(n // nseq,),
            in_specs=page_specs(0) + [const(w1.shape), const(w2.shape), const(hid0.shape)],
            out_specs=[per_seq((n_str, gd))] * 2,
            scratch_shapes=[pltpu.VMEM((2, gd // LANES, nseq * n_pages * page, LANES), F32)],
        ),
        out_shape=[jax.ShapeDtypeStruct((n, n_str, gd), BF16)] * 2,
        compiler_params=_params(("arbitrary",)),
        name="nsa_sample_compress",
    )(pt, *pages, w1, w2, hid0)
    return pl.pallas_call(
        functools.partial(_nsa_sample_attend_kernel, n_pages=n_pages, page=page, tnew=tnew, ns=ns, nseq=nseq),
        grid_spec=pltpu.PrefetchScalarGridSpec(
            num_scalar_prefetch=1,
            grid=(n // nseq,),
            in_specs=page_specs(1) + [
                new_spec(2), new_spec(3), new_spec(4), new_spec(5),
                pl.BlockSpec((1, nseq, 2, gd, wbuf), lambda i, pt: (layer, i, 0, 0, 0)),
                per_seq((rows, gd)), per_seq((rows, 3)), per_seq((n_str, gd)), per_seq((n_str, gd)),
                const(cb.shape), const(sb.shape), const(wb.shape), const(ov.shape), const(hs.shape),
                const(ex.shape),
            ],
            out_specs=per_seq((g, rows // g, dh)),
            scratch_shapes=[pltpu.VMEM((nseq, n_pages + 1, rows, page), F32),
                            pltpu.VMEM((nseq, wbuf // page + 1, rows, page), F32)],
        ),
        out_shape=jax.ShapeDtypeStruct((n, g, rows // g, dh), BF16),
        compiler_params=_params(("arbitrary",)),
        name="nsa_sample_attend",
    )(pt, *pages, p, p, p, p, win_t, q4, gates, kc, vc, cb, sb, wb, ov, hs, ex)


def _nsa_sample_tables(rel_bias, past, tnew, wbuf, page, hpg):
    g = NSA_KV_GROUPS
    h = g * hpg
    qpos = past + jnp.arange(tnew)
    n_str = (past + tnew) // CMP_STRIDE
    ns = -(-(past + tnew) // SEL_BLOCK)
    nsp = LANES
    assert ns <= nsp

    def rows(b):
        return b.reshape(h * tnew, b.shape[2])

    def tiled(b, n_tiles):
        return b.reshape(h * tnew, n_tiles, page).transpose(1, 0, 2)

    cmp_end = jnp.arange(n_str) * CMP_STRIDE + CMP_BLOCK - 1
    dc = qpos[:, None] - cmp_end[None, :]
    cb = rows(_dist_bias(rel_bias, dc, (dc >= 0) & (jnp.arange(n_str) < n_str - 1)[None, :]))
    n_tiles = past // page + 1
    kpos = jnp.arange(n_tiles * page)
    ds = qpos[:, None] - kpos[None, :]
    sb = tiled(rows(_dist_bias(rel_bias, ds, (ds >= 0) & (kpos < past + tnew)[None, :])), n_tiles)
    w_tiles = wbuf // page + 1
    r = jnp.arange(w_tiles * page)
    dw = qpos[:, None] - (past - wbuf + r)[None, :]
    wb = tiled(rows(_dist_bias(rel_bias, dw, (dw >= 0) & (dw < WINDOW) & (r < wbuf + tnew)[None, :])), w_tiles)
    ov = jnp.zeros((n_str, nsp), F32).at[:n_str - 1, :ns].set(_cmp_sel_overlap(n_str - 1, ns))
    row = jnp.arange(h * tnew)
    same = (row[:, None] // (hpg * tnew) == row[None, :] // (hpg * tnew)) & \
           (row[:, None] % tnew == row[None, :] % tnew)
    expand = (kpos[None, :] // SEL_BLOCK == jnp.arange(nsp)[:, None]).astype(BF16)
    expand = expand.reshape(nsp, n_tiles, page).transpose(1, 0, 2)
    return cb, sb, wb, ov, same.astype(F32), expand


def _softplus(x):
    return jnp.maximum(x, 0.0) + jnp.log1p(jnp.exp(-jnp.abs(x)))


def _transpose_rows(x):
    r = x.shape[0]
    if r >= GDN_CHUNK:
        return x.T
    return jnp.concatenate([x, jnp.zeros((LANES - r, x.shape[1]), x.dtype)], axis=0).T[:, 0:r]


def _unit_lower_inverse(a):
    c = a[0].shape[0]
    eye = (lax.broadcasted_iota(jnp.int32, (c, c), 0) == lax.broadcasted_iota(jnp.int32, (c, c), 1)).astype(F32)
    inv = [eye - x for x in a]
    power = a
    n = 2
    while n < c:
        power = [_dot(x, x) for x in power]
        inv = [y + _dot(y, x) for x, y in zip(power, inv)]
        n *= 2
    resid = [(eye - y) - _dot_split(x, y) for x, y in zip(a, inv)]
    return [y + _dot(y, r) for y, r in zip(inv, resid)]


def _gdn_kernel(qkv_ref, z_ref, ab_ref, hist_ref, s0_ref, cw_ref, nar_ref, dtr_ref, nac_ref, dtc_ref,
                nw_ref, *rest, c, tv, qk_heads, v_heads, n_prev):
    prev_refs = rest[:n_prev]
    o_ref, sfin_ref, s_scr, carry_scr = rest[n_prev:]
    hd = GDN_HEAD
    halo = carry_scr.shape[0] - tv
    step = pl.program_id(1)

    @pl.when(step == 0)
    def _():
        s_scr[...] = s0_ref[0, 0]
        carry_scr[pl.ds(0, halo), :] = hist_ref[0]

    carry_scr[pl.ds(halo, tv), :] = qkv_ref[...]
    ext = carry_scr[...]
    conv = None
    for i in range(GDN_CONV):
        shift = GDN_CONV - 1 - i
        tap = (pltpu.roll(ext, shift, axis=0) if shift else ext)[halo:halo + tv] * cw_ref[i:i + 1]
        conv = tap if conv is None else conv + tap
    carry_scr[pl.ds(0, halo), :] = carry_scr[pl.ds(tv, halo), :]
    act = conv * jax.nn.sigmoid(conv)
    if tv < c:
        act = jnp.concatenate([act, jnp.zeros((c - tv, act.shape[1]), F32)], axis=0)
        ab = jnp.concatenate([ab_ref[...], jnp.zeros((c - tv, ab_ref.shape[1]), F32)], axis=0)
    else:
        ab = ab_ref[...]

    row = lax.broadcasted_iota(jnp.int32, (c, c), 0)
    col = lax.broadcasted_iota(jnp.int32, (c, c), 1)
    incl = row >= col
    strict = row > col
    lower = incl.astype(F32)
    upper = (row <= col).astype(F32)

    live_c = lax.broadcasted_iota(jnp.int32, ab.shape, 0) < tv
    g_col = jnp.where(live_c, nar_ref[...] * _softplus(ab + dtr_ref[...]), 0.0)
    beta = jnp.where(live_c, jax.nn.sigmoid(ab), 0.0)
    abt = _transpose_rows(ab)[0:v_heads]
    live_r = lax.broadcasted_iota(jnp.int32, abt.shape, 1) < tv
    g_row = jnp.where(live_r, nac_ref[...] * _softplus(abt + dtc_ref[...]), 0.0)
    gam_col = _dot_f32(lower, g_col)
    gam_row = _dot_f32(g_row, upper)

    rep = v_heads // qk_heads
    hvs = range(v_heads)
    qn, kn, kk, qk = [], [], [], []
    for hq in range(qk_heads):
        qh = act[:, hq * hd:(hq + 1) * hd]
        kh = act[:, (qk_heads + hq) * hd:(qk_heads + hq + 1) * hd]
        qn.append(qh * lax.rsqrt(jnp.sum(qh * qh, axis=-1, keepdims=True) + 1e-6) * (hd ** -0.5))
        kn.append(kh * lax.rsqrt(jnp.sum(kh * kh, axis=-1, keepdims=True) + 1e-6))
    for hq in range(qk_heads):
        kk.append(_dot_nt(kn[hq], kn[hq]))
        qk.append(_dot_nt(qn[hq], kn[hq]))
    gc = [gam_col[:, hv:hv + 1] for hv in hvs]
    bc = [beta[:, v_heads + hv:v_heads + hv + 1] for hv in hvs]
    decay = [jnp.where(incl, jnp.exp(jnp.where(incl, gc[hv] - gam_row[hv:hv + 1, :], 0.0)), 0.0) for hv in hvs]
    a_mat = [jnp.where(strict, bc[hv] * kk[hv // rep] * decay[hv], 0.0) for hv in hvs]
    tinv = _unit_lower_inverse(a_mat)
    eg = [jnp.exp(gc[hv]) for hv in hvs]
    g_last = [gc[hv][c - 1:c, :] for hv in hvs]
    rhs = [jnp.concatenate([act[:, (2 * qk_heads + hv) * hd:(2 * qk_heads + hv + 1) * hd] * bc[hv],
                            kn[hv // rep] * (bc[hv] * eg[hv])], axis=1) for hv in hvs]
    vk = [_dot(tinv[hv], rhs[hv]) for hv in hvs]
    s_old = [s_scr[hv] for hv in hvs]
    ks = [_dot(jnp.concatenate([vk[hv][:, hd:], qn[hv // rep] * eg[hv]], axis=0), s_old[hv]) for hv in hvs]
    u = [vk[hv][:, :hd] - ks[hv][:c] for hv in hvs]
    k_dec = [_transpose_rows(kn[hv // rep] * jnp.exp(g_last[hv] - gc[hv])) for hv in hvs]
    upd = [_dot(jnp.concatenate([qk[hv // rep] * decay[hv], k_dec[hv]], axis=0), u[hv]) for hv in hvs]
    for hv in hvs:
        s_scr[hv] = s_old[hv] * jnp.exp(g_last[hv]) + upd[hv][c:]
        o = (ks[hv][c:] + upd[hv][:c])[0:tv]
        zh = z_ref[:, hv * hd:(hv + 1) * hd]
        o = _rms(o, nw_ref[...]) * (zh * jax.nn.sigmoid(zh))
        o_ref[:, hv * hd:(hv + 1) * hd] = o.astype(BF16)

    @pl.when(step == pl.num_programs(1) - 1)
    def _():
        for l, prev in enumerate(prev_refs):
            sfin_ref[l, 0] = prev[0]
        sfin_ref[n_prev, 0] = s_scr[...]


def gdn(p, row0, n, t, hist, s0, layer, conv_w, a_log, dt_bias, norm_w, *, qk_heads, v_heads, prev=()):
    hd = GDN_HEAD
    ch = (2 * qk_heads + v_heads) * hd
    vw = v_heads * hd
    tv = min(GDN_CHUNK, t)
    c = max(tv, 16)
    assert t % tv == 0 and row0 % tv == 0 and ch % vw == 0
    nchunk = t // tv
    blk0 = row0 // tv
    halo = 8
    histp = jnp.concatenate([jnp.zeros((n, halo - (GDN_CONV - 1), ch), F32), hist.astype(F32)], axis=1)
    cw = jnp.concatenate([conv_w.astype(F32), jnp.zeros((halo - GDN_CONV, ch), F32)], axis=0)
    neg_a = -jnp.exp(a_log.astype(F32))
    dt = dt_bias.astype(F32)
    pad = jnp.zeros((LANES - v_heads,), F32)
    nar = jnp.concatenate([neg_a, pad]).reshape(1, LANES)
    dtr = jnp.concatenate([dt, pad]).reshape(1, LANES)
    n_prev = len(prev)
    state_spec = pl.BlockSpec((1, v_heads, hd, hd), lambda b, s: (b, 0, 0, 0))
    o, sfin = pl.pallas_call(
        functools.partial(_gdn_kernel, c=c, tv=tv, qk_heads=qk_heads, v_heads=v_heads, n_prev=n_prev),
        grid=(n, nchunk),
        in_specs=[
            pl.BlockSpec((tv, ch), lambda b, s: (blk0 + b * nchunk + s, 0)),
            pl.BlockSpec((tv, vw), lambda b, s: (blk0 + b * nchunk + s, ch // vw)),
            pl.BlockSpec((tv, LANES), lambda b, s: (blk0 + b * nchunk + s, (ch + vw) // LANES)),
            pl.BlockSpec((1, halo, ch), lambda b, s: (b, 0, 0)),
            pl.BlockSpec((1, 1, v_heads, hd, hd), lambda b, s: (layer, b, 0, 0, 0)),
            pl.BlockSpec((halo, ch), lambda b, s: (0, 0)),
            pl.BlockSpec((1, LANES), lambda b, s: (0, 0)),
            pl.BlockSpec((1, LANES), lambda b, s: (0, 0)),
            pl.BlockSpec((v_heads, 1), lambda b, s: (0, 0)),
            pl.BlockSpec((v_heads, 1), lambda b, s: (0, 0)),
            pl.BlockSpec((1, hd), lambda b, s: (0, 0)),
        ] + [state_spec] * n_prev,
        out_specs=[
            pl.BlockSpec((tv, vw), lambda b, s: (b * nchunk + s, 0)),
            pl.BlockSpec((n_prev + 1, 1, v_heads, hd, hd), lambda b, s: (0, b, 0, 0, 0)),
        ],
        out_shape=[jax.ShapeDtypeStruct((n * t, vw), BF16),
                   jax.ShapeDtypeStruct((n_prev + 1, n, v_heads, hd, hd), F32)],
        scratch_shapes=[pltpu.VMEM((v_heads, hd, hd), F32), pltpu.VMEM((halo + tv, ch), F32)],
        compiler_params=_params(("parallel", "arbitrary")),
        name="gdn",
    )(p, p, p, histp, s0.astype(F32), cw, nar, dtr, neg_a.reshape(v_heads, 1), dt.reshape(v_heads, 1),
      norm_w.astype(F32).reshape(1, hd), *prev)
    return o, sfin


def _pow2_divisor(m, cap):
    tile = cap
    while m % tile:
        tile //= 2
    return tile


def _window(p, r0, r1, c0, c1):
    return lax.slice(p, (r0, c0), (r1, c1))


def _pad_cols(w, mult):
    n = w.shape[-1]
    return jnp.pad(w, [(0, 0)] * (w.ndim - 1) + [(0, -n % mult)])


def kernel(x_prompt, x_sample, cache_kv, cache_win, state_ssm, state_conv, page_table, rel_bias, norm_mix, norm_ffn, norm_final, nsa_w_in, nsa_w_out, nsa_cmp_w1, nsa_cmp_w2, nsa_cmp_pe, gdn_w_in, gdn_conv_w, gdn_a_log, gdn_dt_bias, gdn_norm_w, gdn_w_out, ffn_w1, ffn_w2):
    nb, t, d = x_prompt.shape
    ns_, tnew, _ = x_sample.shape
    depth = norm_mix.shape[0]
    g, dh = NSA_KV_GROUPS, NSA_HEAD_DIM
    gd = g * dh
    heads = nsa_w_out.shape[1] // dh
    hpg = heads // g
    qw = heads * dh
    mp = nb * t
    past = page_table.shape[1] * cache_kv.shape[2]
    wbuf = cache_win.shape[2]
    page = cache_kv.shape[2]
    v_heads = gdn_a_log.shape[1]
    qk_heads = (gdn_conv_w.shape[2] // GDN_HEAD - v_heads) // 2
    conv_ch = gdn_conv_w.shape[2]
    m_all = nb * t + ns_ * tnew
    tn_nsa = -(-nsa_w_in.shape[2] // LANES) * LANES
    tn_gdn = 10 * LANES
    tm_nsa = _pow2_divisor(m_all, 1024)
    tm_gdn = _pow2_divisor(m_all, 1024)
    tm_ffn = _pow2_divisor(math.gcd(mp, ns_ * tnew), 512)

    x = jnp.concatenate([x_prompt.reshape(mp, d), x_sample.reshape(ns_ * tnew, d)], axis=0).astype(F32)
    tq = LANES
    p_tables = _nsa_prompt_t_tables(rel_bias, t, tq, hpg)
    s_tables = _nsa_sample_tables(rel_bias, past, tnew, wbuf, page, hpg)
    cache_t = cache_kv.transpose(0, 1, 3, 4, 5, 2).reshape(cache_kv.shape[0], cache_kv.shape[1], NSA_CACHED, gd, page)
    win_t = cache_win.transpose(0, 1, 3, 4, 5, 2).reshape(cache_win.shape[0], ns_, 2, gd, wbuf)

    nsa_w_in_b = _pad_cols(nsa_w_in, tn_nsa).astype(BF16)
    gdn_w_in_b = _pad_cols(gdn_w_in, tn_gdn).astype(BF16)
    nsa_w_out_b, gdn_w_out_b = nsa_w_out.astype(BF16), gdn_w_out.astype(BF16)
    ffn_w1_b, ffn_w2_b = ffn_w1.astype(BF16), ffn_w2.astype(BF16)
    gains_mix = norm_mix.astype(F32).reshape(depth, 1, d)
    gains_ffn = norm_ffn.astype(F32).reshape(depth, 1, d)

    kv_p, kv_s, win_p, win_s, ssm_p, ssm_s, conv_p, conv_s = [], [], [], [], [], [], [], []
    for i in range(depth):
        li = i // 2
        if i % 2 == 0:
            p = norm_matmul(x, gains_mix, i, nsa_w_in_b, li, tm=tm_nsa, tn=tn_nsa)
            kv_end = qw + NSA_N_KV * gd
            ps = _window(p, mp, m_all, 0, kv_end + 3 * heads).reshape(ns_, tnew, -1)
            cweights = _compress_weights(nsa_cmp_w1[li], nsa_cmp_w2[li], nsa_cmp_pe[li])
            kc, vct = nsa_compress(p, nb, t, qw, *cweights)
            kv4 = _window(p, 0, mp, qw + 2 * gd, kv_end).reshape(nb, t, 2, 2, g, dh)
            kt = kv4[:, :, :, 0].transpose(2, 0, 3, 1, 4).astype(BF16)
            vt = kv4[:, :, :, 1].transpose(2, 0, 3, 4, 1).astype(BF16)
            aug = jnp.zeros((2, nb, g, V_AUG, t), BF16).at[:, :, :, 0].set(1.0)
            vt = jnp.concatenate([vt, aug], axis=3).reshape(2, nb, g, dh + V_AUG, t // tq, tq)
            vt = vt.transpose(0, 1, 2, 4, 3, 5)
            gates_p = _window(p, 0, mp, kv_end, kv_end + 3 * heads).reshape(mp, g, 3 * hpg).transpose(1, 2, 0)
            o_p = nsa_prompt_t(p, gates_p, kc, vct, kt, vt, *p_tables, nb, t, tq=tq, sub=4)
            q_s = (ps[:, :, :qw] * dh ** -0.5).reshape(ns_, tnew, g, hpg, dh)
            q4 = jnp.einsum("ntghd,gk->nghtkd", q_s, jnp.eye(g, dtype=F32))
            q4 = q4.reshape(ns_, heads * tnew, gd).astype(BF16)
            gates_s = ps[:, :, kv_end:kv_end + 3 * heads].reshape(ns_, tnew, heads, 3)
            gates_s = gates_s.transpose(0, 2, 1, 3).reshape(ns_, heads * tnew, 3)
            o_s = nsa_sample(p, mp, cache_t, li, win_t, page_table, q4, gates_s, s_tables, cweights, tnew=tnew)
            o_s = o_s.reshape(ns_, g, hpg, tnew, dh).transpose(0, 3, 1, 2, 4).reshape(ns_ * tnew, qw)
            w_outs = nsa_w_out_b
            kv_p.append(_window(p, 0, mp, qw, qw + NSA_CACHED * gd).reshape(nb, t, NSA_CACHED, g, dh))
            kv_s.append(ps[:, :, qw:qw + NSA_CACHED * gd].reshape(ns_, tnew, NSA_CACHED, g, dh))
            wkeep = min(WINDOW, t)
            win_p.append(jnp.stack([_window(p, b * t + t - wkeep, (b + 1) * t, qw + NSA_CACHED * gd, kv_end)
                                    for b in range(nb)]).reshape(nb, wkeep, 2, g, dh))
            win_s.append(ps[:, :, qw + NSA_CACHED * gd:kv_end].reshape(ns_, tnew, 2, g, dh))
        else:
            p = norm_matmul(x, gains_mix, i, gdn_w_in_b, li, tm=tm_gdn, tn=tn_gdn)
            args = (gdn_conv_w[li], gdn_a_log[li], gdn_dt_bias[li], gdn_norm_w[li])
            h0 = jnp.zeros((nb, GDN_CONV - 1, conv_ch), F32)
            s0 = jnp.zeros((1, nb, v_heads, GDN_HEAD, GDN_HEAD), F32)
            last = i + 2 >= depth
            o_p, sp = gdn(p, 0, nb, t, h0, s0, 0, *args, qk_heads=qk_heads, v_heads=v_heads,
                          prev=tuple(ssm_p) if last else ())
            o_s, ss = gdn(p, mp, ns_, tnew, state_conv[li], state_ssm, li, *args,
                          qk_heads=qk_heads, v_heads=v_heads, prev=tuple(ssm_s) if last else ())
            sp, ss = (sp, ss) if last else (sp[0], ss[0])
            w_outs = gdn_w_out_b
            ssm_p.append(sp.astype(state_ssm.dtype))
            ssm_s.append(ss.astype(state_ssm.dtype))
            keep = GDN_CONV - 1
            assert t >= keep and tnew >= keep
            conv_p.append(jnp.stack([_window(p, (b + 1) * t - keep, (b + 1) * t, 0, conv_ch) for b in range(nb)]))
            conv_s.append(_window(p, mp, m_all, 0, conv_ch).reshape(ns_, tnew, conv_ch)[:, tnew - keep:])
        x = mix_ffn(x, o_p, o_s, w_outs, li, gains_ffn, ffn_w1_b, ffn_w2_b, i, norm_final,
                    final_norm=(i == depth - 1), tm=tm_ffn, th=1024)
    y_prompt = x[:mp].reshape(nb, t, d)
    y_sample = x[mp:].reshape(ns_, tnew, d)
    new_win = jnp.stack(win_s).astype(cache_win.dtype)
    win_s_all = jnp.concatenate([cache_win[:new_win.shape[0]], new_win], axis=2)[:, :, -wbuf:]
    return (y_prompt, y_sample, jnp.stack(kv_p), jnp.stack(kv_s), jnp.stack(win_p), win_s_all,
            ssm_p[-1], ssm_s[-1], jnp.stack(conv_p), jnp.stack(conv_s))
```
